```python
import jax, jax.numpy as jnp
from jax import lax
import numpy as np


D_MODEL = 1024
BATCH = 2
SEQ = 16384
DEPTH = 2

HEAD_DIM = 64
N_GROUPS = 4
GROUP_WIDTH = D_MODEL // N_GROUPS
D_MIX = N_GROUPS * GROUP_WIDTH
A_Q_HEADS = GROUP_WIDTH // HEAD_DIM
A_KV_HEADS = 2
D_Q_HEADS = GROUP_WIDTH // HEAD_DIM
D_KV_HEADS = 2
B_HEADS = GROUP_WIDTH // HEAD_DIM
C_HEADS = GROUP_WIDTH // HEAD_DIM
CONV_WIDTH = 3
WINDOW = 128
BLOCK = 128
GRID_W = 64
AXIS_DIM = HEAD_DIM // 2
ROPE_THETA = 10000.0
D_FF = 4 * D_MODEL
LN_EPS = 1e-5
RMS_EPS = 1e-6
DN_ALPHA = (2 * DEPTH) ** 0.25
DN_BETA = (8 * DEPTH) ** -0.25
NEG_INF = -1e30
SPLIT_SIZES = (
    A_Q_HEADS * HEAD_DIM, A_KV_HEADS * HEAD_DIM, A_KV_HEADS * HEAD_DIM,
    GROUP_WIDTH, GROUP_WIDTH, GROUP_WIDTH,
    GROUP_WIDTH,
    D_Q_HEADS * HEAD_DIM, D_KV_HEADS * HEAD_DIM, D_KV_HEADS * HEAD_DIM,
)
D_IN_PROJ = 2048

kernel_name = 'hymba_style_hybrid_encoder'


def layer_norm(x, g, b):
    xf = x.astype(jnp.float32)
    mu = jnp.mean(xf, axis=-1, keepdims=True)
    var = jnp.mean(jnp.square(xf - mu), axis=-1, keepdims=True)
    return ((xf - mu) * lax.rsqrt(var + LN_EPS) * g + b).astype(x.dtype)


def rms_norm(x, g):
    xf = x.astype(jnp.float32)
    return (xf * lax.rsqrt(jnp.mean(jnp.square(xf), axis=-1, keepdims=True) + RMS_EPS) * g).astype(x.dtype)


def split_columns(h):
    parts, start = [], 0
    for size in SPLIT_SIZES:
        parts.append(h[..., start:start + size])
        start += size
    return parts


def split_heads(t, n_heads):
    return t.reshape(t.shape[0], t.shape[1], n_heads, HEAD_DIM)


def windowed_gqa_sink(q, k, v, sink):
    b_, s_, hq, hd = q.shape
    hkv = k.shape[2]
    g_ = hq // hkv
    nblk = s_ // BLOCK
    qb = q.reshape(b_, nblk, BLOCK, hkv, g_, hd)

    def windows(t):
        tp = jnp.pad(t, ((0, 0), (BLOCK, BLOCK), (0, 0), (0, 0))).reshape(b_, nblk + 2, BLOCK, hkv, hd)
        return jnp.concatenate([tp[:, :-2], tp[:, 1:-1], tp[:, 2:]], axis=2)

    kw, vw = windows(k), windows(v)
    scores = jnp.einsum('bnqkgd,bnskd->bnkgqs', qb, kw, preferred_element_type=jnp.float32) * (hd ** -0.5)
    q_pos = jnp.arange(s_).reshape(nblk, BLOCK)
    k_pos = (jnp.arange(nblk)[:, None] - 1) * BLOCK + jnp.arange(3 * BLOCK)[None, :]
    dist = jnp.abs(q_pos[:, :, None] - k_pos[:, None, :])
    valid = (dist <= WINDOW) & ((k_pos >= 0) & (k_pos < s_))[:, None, :]
    slopes = jnp.exp2(-8.0 * jnp.arange(1, hq + 1, dtype=jnp.float32) / hq).reshape(hkv, g_)
    bias = -slopes[None, :, :, None, None] * dist.astype(jnp.float32)[:, None, None, :, :]
    scores = jnp.where(valid[:, None, None], scores + bias, NEG_INF)
    sink_col = jnp.broadcast_to(sink.astype(jnp.float32).reshape(hkv, g_, 1, 1), scores.shape[:-1] + (1,))
    probs = jax.nn.softmax(jnp.concatenate([scores, sink_col], axis=-1), axis=-1)[..., :-1]
    out = jnp.einsum('bnkgqs,bnskd->bnqkgd', probs.astype(v.dtype), vw)
    return out.reshape(b_, s_, hq * hd)


def short_conv_mixer(u, gate_b, gate_c, conv_w):
    z = gate_c * u
    y = lax.conv_general_dilated(
        z, conv_w[:, None, :], window_strides=(1,),
        padding=((CONV_WIDTH // 2, CONV_WIDTH // 2),),
        dimension_numbers=('NWC', 'WIO', 'NWC'),
        feature_group_count=z.shape[-1])
    return gate_b * y


def fourier_mixer(u):
    b_, s_, c_ = u.shape
    ug = u.astype(jnp.float32).reshape(b_, s_, C_HEADS, c_ // C_HEADS)
    y = jnp.fft.fft2(ug, axes=(1, 3), norm='ortho').real
    return y.reshape(b_, s_, c_).astype(u.dtype)


def rotate_half_axis(seg, ang):
    m = ang.shape[-1]
    c = jnp.cos(ang)[None, :, None, :]
    s = jnp.sin(ang)[None, :, None, :]
    x1, x2 = seg[..., :m], seg[..., m:]
    return jnp.concatenate([x1 * c - x2 * s, x2 * c + x1 * s], axis=-1)


def axial_rope(t, ang_row, ang_col):
    tf = t.astype(jnp.float32)
    out = jnp.concatenate([rotate_half_axis(tf[..., :AXIS_DIM], ang_row),
                           rotate_half_axis(tf[..., AXIS_DIM:], ang_col)], axis=-1)
    return out.astype(t.dtype)


def dense_gqa_blocks(q, k, v):
    b_, s_, hq, hd = q.shape
    hkv = k.shape[2]
    g_ = hq // hkv
    nblk = s_ // BLOCK
    qb = q.reshape(b_, nblk, BLOCK, hkv, g_, hd).transpose(1, 0, 2, 3, 4, 5)

    def one_block(q_blk):
        sc = jnp.einsum('bqkgd,bskd->bkgqs', q_blk, k, preferred_element_type=jnp.float32) * (hd ** -0.5)
        p = jax.nn.softmax(sc, axis=-1)
        return jnp.einsum('bkgqs,bskd->bqkgd', p.astype(v.dtype), v)

    out = lax.map(one_block, qb)
    return out.transpose(1, 0, 2, 3, 4, 5).reshape(b_, s_, hq * hd)


def setup_inputs(seed: int = 0) -> dict:
    key = jax.random.key(seed)
    ks = jax.random.split(key, 16)
    f32 = jnp.float32
    nrm = lambda k, shape: jax.random.normal(k, shape, dtype=f32)
    return {
        'x': nrm(ks[0], (BATCH, SEQ, D_MODEL)),
        'ln_in_g': 1.0 + 0.02 * nrm(ks[1], (D_MODEL,)),
        'ln_in_b': 0.02 * nrm(ks[2], (D_MODEL,)),
        'w_in': nrm(ks[3], (DEPTH, D_MODEL, D_IN_PROJ)) * D_MODEL ** -0.5,
        'conv_w': nrm(ks[4], (DEPTH, CONV_WIDTH, GROUP_WIDTH)) * CONV_WIDTH ** -0.5,
        'sink': 0.5 * nrm(ks[5], (DEPTH, A_Q_HEADS)),
        'qn_g': 1.0 + 0.02 * nrm(ks[6], (DEPTH, HEAD_DIM)),
        'kn_g': 1.0 + 0.02 * nrm(ks[7], (DEPTH, HEAD_DIM)),
        'grp_g': 1.0 + 0.02 * nrm(ks[8], (DEPTH, D_MIX)),
        'w_out': nrm(ks[9], (DEPTH, D_MIX, D_MODEL)) * (D_MIX ** -0.5) * DN_BETA,
        'ln1_g': 1.0 + 0.02 * nrm(ks[10], (DEPTH, D_MODEL)),
        'ln1_b': 0.02 * nrm(ks[11], (DEPTH, D_MODEL)),
        'w1': nrm(ks[12], (DEPTH, D_MODEL, D_FF)) * (D_MODEL ** -0.5) * DN_BETA,
        'w2': nrm(ks[13], (DEPTH, D_FF, D_MODEL)) * (D_FF ** -0.5) * DN_BETA,
        'ln2_g': 1.0 + 0.02 * nrm(ks[14], (DEPTH, D_MODEL)),
        'ln2_b': 0.02 * nrm(ks[15], (DEPTH, D_MODEL)),
    }


def reference(x, ln_in_g, ln_in_b, w_in, conv_w, sink, qn_g, kn_g, grp_g, w_out,
              ln1_g, ln1_b, w1, w2, ln2_g, ln2_b):
    f32 = jnp.float32
    b_, s_, _ = x.shape
    rows = s_ // GRID_W
    row_idx = jnp.repeat(jnp.arange(rows), GRID_W)
    col_idx = jnp.tile(jnp.arange(GRID_W), rows)
    inv_freq = ROPE_THETA ** (-jnp.arange(0, AXIS_DIM, 2, dtype=f32) / AXIS_DIM)
    ang_row = row_idx.astype(f32)[:, None] * inv_freq[None, :]
    ang_col = col_idx.astype(f32)[:, None] * inv_freq[None, :]

    h = layer_norm(x, ln_in_g, ln_in_b)
    for l in range(DEPTH):
        proj = h @ w_in[l]
        a_q, a_k, a_v, b_u, b_b, b_c, c_u, d_q, d_k, d_v = split_columns(proj)
        out_a = windowed_gqa_sink(split_heads(a_q, A_Q_HEADS), split_heads(a_k, A_KV_HEADS),
                                  split_heads(a_v, A_KV_HEADS), sink[l])
        out_b = short_conv_mixer(b_u, b_b, b_c, conv_w[l])
        out_c = fourier_mixer(c_u)
        qd = axial_rope(rms_norm(split_heads(d_q, D_Q_HEADS), qn_g[l]), ang_row, ang_col)
        kd = axial_rope(rms_norm(split_heads(d_k, D_KV_HEADS), kn_g[l]), ang_row, ang_col)
        out_d = dense_gqa_blocks(qd, kd, split_heads(d_v, D_KV_HEADS))
        g = grp_g[l]
        mix = jnp.concatenate(
            [rms_norm(o, g[i * GROUP_WIDTH:(i + 1) * GROUP_WIDTH])
             for i, o in enumerate((out_a, out_b, out_c, out_d))], axis=-1)
        h = layer_norm(DN_ALPHA * h + mix @ w_out[l], ln1_g[l], ln1_b[l])
        ffn = jnp.square(jax.nn.relu(h @ w1[l])) @ w2[l]
        h = layer_norm(DN_ALPHA * h + ffn, ln2_g[l], ln2_b[l])
    return h
```

```python
import functools
import math

import numpy as np
import jax
import jax.numpy as jnp
from jax import lax
from jax.experimental import pallas as pl
from jax.experimental.pallas import tpu as pltpu

F32 = jnp.float32
BF16 = jnp.bfloat16

D_MODEL = 1024
HEAD_DIM = 64
GROUP_WIDTH = 256
Q_HEADS = 4
KV_HEADS = 2
KV_WIDTH = KV_HEADS * HEAD_DIM
CONV_WIDTH = 3
WINDOW = 128
BLOCK = 128
GRID_W = 64
AXIS_DIM = HEAD_DIM // 2
ROPE_THETA = 10000.0
D_FF = 4 * D_MODEL
LN_EPS = 1e-5
RMS_EPS = 1e-6
DEPTH = 2
DN_ALPHA = (2 * DEPTH) ** 0.25
NEG_INF = -1e30
ALIBI_SLOPES = tuple(2.0 ** (-8.0 * (h + 1) / Q_HEADS) for h in range(Q_HEADS))

V7X_VMEM_LIMIT_BYTES = 56 * 1024 * 1024
LANES = 128
SUBLANES = 8


def _params(*semantics):
    return pltpu.CompilerParams(dimension_semantics=semantics,
                                vmem_limit_bytes=V7X_VMEM_LIMIT_BYTES)


def _resident(block_shape, index_map):
    return pl.BlockSpec(block_shape, index_map, pipeline_mode=pl.Buffered(1))


def _dot(a, b):
    return jnp.dot(a, b, preferred_element_type=F32)


def _split_bf16(x):
    hi = x.astype(BF16)
    lo = (x - hi.astype(F32)).astype(BF16)
    return hi, lo


def _dot3(a_hi, a_lo, b_hi, b_lo):
    return _dot(a_hi, b_hi) + _dot(a_hi, b_lo) + _dot(a_lo, b_hi)


def _layer_norm(x, g, b):
    mu = jnp.mean(x, axis=-1, keepdims=True)
    xc = x - mu
    var = jnp.mean(xc * xc, axis=-1, keepdims=True)
    return xc * lax.rsqrt(var + LN_EPS) * g + b


def _group_rms_norm(x, g):
    ms = jnp.mean(x * x, axis=-1, keepdims=True)
    return x * lax.rsqrt(ms + RMS_EPS) * g


A_COLS = 512
B_COLS = 768
C_COLS = 256
D_COLS = 512
D_IN_PROJ = A_COLS + B_COLS + C_COLS + D_COLS


def _in_proj_kernel(apply_ln, x_ref, g_ref, b_ref, w_ref, *out_refs):
    x = x_ref[...]
    if apply_ln:
        h_ref, pa_ref, pb_ref, pc_ref, pd_ref = out_refs
        x = _layer_norm(x, g_ref[...], b_ref[...])
        h_ref[...] = x
    else:
        pa_ref, pb_ref, pc_ref, pd_ref = out_refs
    proj = _dot(x.astype(BF16), w_ref[...])
    pa_ref[...] = proj[:, :A_COLS]
    pb_ref[...] = proj[:, A_COLS:A_COLS + B_COLS]
    pc_ref[...] = proj[:, A_COLS + B_COLS:A_COLS + B_COLS + C_COLS]
    pd_ref[...] = proj[:, A_COLS + B_COLS + C_COLS:]


def _in_proj(x, g, b, w, apply_ln):
    rows = x.shape[0]
    tm = min(512, rows)
    row_spec = lambda width: pl.BlockSpec((tm, width), lambda i: (i, 0))
    const = lambda shape: _resident(shape, lambda i: (0, 0))
    widths = ([D_MODEL] if apply_ln else []) + [A_COLS, B_COLS, C_COLS, D_COLS]
    return pl.pallas_call(
        functools.partial(_in_proj_kernel, apply_ln),
        grid=(rows // tm,),
        in_specs=[row_spec(D_MODEL), const((1, D_MODEL)), const((1, D_MODEL)),
                  const((D_MODEL, D_IN_PROJ))],
        out_specs=[row_spec(wd) for wd in widths],
        out_shape=[jax.ShapeDtypeStruct((rows, wd), F32) for wd in widths],
        compiler_params=_params("parallel"),
        name="in_proj_ln" if apply_ln else "in_proj",
    )(x, g, b, w)


def _window_attn_kernel(seq_len, q_ref, km_ref, kp_ref, kn_ref, vm_ref, vp_ref, vn_ref,
                        sink_ref, g_ref, o_ref, kall_ref, vall_ref):
    tq = q_ref.shape[0]
    n_blk = tq // BLOCK
    tile_start = pl.program_id(1) * tq

    kall_ref[0:BLOCK, :] = kp_ref[...].astype(BF16)
    kall_ref[BLOCK:BLOCK + tq, :] = km_ref[...].astype(BF16)
    kall_ref[BLOCK + tq:, :] = kn_ref[...].astype(BF16)
    vall_ref[0:BLOCK, :] = vp_ref[...].astype(BF16)
    vall_ref[BLOCK:BLOCK + tq, :] = vm_ref[...].astype(BF16)
    vall_ref[BLOCK + tq:, :] = vn_ref[...].astype(BF16)

    r_idx = lax.broadcasted_iota(jnp.int32, (BLOCK, 3 * BLOCK), 0)
    c_idx = lax.broadcasted_iota(jnp.int32, (BLOCK, 3 * BLOCK), 1)
    dist = jnp.abs(BLOCK + r_idx - c_idx)
    in_window = dist <= WINDOW
    dist_f = dist.astype(F32)
    lane = lax.broadcasted_iota(jnp.int32, (BLOCK, LANES), 1)
    low_half = lane < HEAD_DIM
    g = g_ref[...]

    for jb in range(n_blk):
        kw = kall_ref[jb * BLOCK:(jb + 3) * BLOCK, :]
        vw = vall_ref[jb * BLOCK:(jb + 3) * BLOCK, :]
        key_pos = tile_start + (jb - 1) * BLOCK + c_idx
        valid = in_window & (key_pos >= 0) & (key_pos < seq_len)
        q_blk = q_ref[jb * BLOCK:(jb + 1) * BLOCK, :] * (HEAD_DIM ** -0.5)
        q_lo, q_hi = q_blk[:, :LANES], q_blk[:, LANES:]
        q_heads = (
            jnp.where(low_half, q_lo, 0.0),
            jnp.where(low_half, pltpu.roll(q_lo, HEAD_DIM, axis=1), 0.0),
            jnp.where(low_half, 0.0, pltpu.roll(q_hi, HEAD_DIM, axis=1)),
            jnp.where(low_half, 0.0, q_hi),
        )
        outs = []
        for h in range(Q_HEADS):
            s = lax.dot_general(q_heads[h].astype(BF16), kw, (((1,), (1,)), ((), ())),
                                preferred_element_type=F32)
            s = jnp.where(valid, s - ALIBI_SLOPES[h] * dist_f, NEG_INF)
            sink = sink_ref[h]
            m = jnp.maximum(jnp.max(s, axis=-1, keepdims=True), sink)
            e = jnp.exp(s - m)
            den = jnp.sum(e, axis=-1, keepdims=True) + jnp.exp(sink - m)
            outs.append(_dot(e.astype(BF16), vw) / den)
        out_lo = jnp.where(low_half, outs[0], pltpu.roll(outs[1], HEAD_DIM, axis=1))
        out_hi = jnp.where(low_half, pltpu.roll(outs[2], HEAD_DIM, axis=1), outs[3])
        out = jnp.concatenate([out_lo, out_hi], axis=-1)
        o_ref[jb * BLOCK:(jb + 1) * BLOCK, :] = _group_rms_norm(out, g).astype(o_ref.dtype)


def _window_attn(pa, sink, g, batch, seq_len):
    tq = min(512, seq_len)
    per_tile = tq // BLOCK
    n_blocks = seq_len // BLOCK
    pa3 = pa.reshape(batch, seq_len, A_COLS)
    k_col, v_col = GROUP_WIDTH // KV_WIDTH, GROUP_WIDTH // KV_WIDTH + 1

    def main(col):
        return pl.BlockSpec((None, tq, KV_WIDTH), lambda b, i: (b, i, col))

    def prev(col):
        return pl.BlockSpec((None, BLOCK, KV_WIDTH),
                            lambda b, i: (b, jnp.maximum(i * per_tile - 1, 0), col))

    def nxt(col):
        return pl.BlockSpec((None, BLOCK, KV_WIDTH),
                            lambda b, i: (b, jnp.minimum((i + 1) * per_tile, n_blocks - 1), col))

    return pl.pallas_call(
        functools.partial(_window_attn_kernel, seq_len),
        grid=(batch, seq_len // tq),
        in_specs=[pl.BlockSpec((None, tq, GROUP_WIDTH), lambda b, i: (b, i, 0)),
                  main(k_col), prev(k_col), nxt(k_col),
                  main(v_col), prev(v_col), nxt(v_col),
                  pl.BlockSpec(memory_space=pltpu.SMEM),
                  _resident((1, GROUP_WIDTH), lambda b, i: (0, 0))],
        out_specs=pl.BlockSpec((None, tq, GROUP_WIDTH), lambda b, i: (b, i, 0)),
        out_shape=jax.ShapeDtypeStruct((batch, seq_len, GROUP_WIDTH), BF16),
        scratch_shapes=[pltpu.VMEM((tq + 2 * BLOCK, KV_WIDTH), BF16),
                        pltpu.VMEM((tq + 2 * BLOCK, KV_WIDTH), BF16)],
        compiler_params=_params("parallel", "parallel"),
        name="window_attn",
    )(pa3, pa3, pa3, pa3, pa3, pa3, pa3, sink, g)


def _short_conv_kernel(u_ref, gb_ref, gc_ref, up_ref, un_ref, cp_ref, cn_ref, w_ref, g_ref,
                       o_ref):
    tm = u_ref.shape[0]
    i = pl.program_id(1)
    last = pl.num_programs(1) - 1
    z = gc_ref[...] * u_ref[...]
    z_before = jnp.where(i > 0, cp_ref[SUBLANES - 1:SUBLANES, :] * up_ref[SUBLANES - 1:SUBLANES, :],
                         0.0)
    z_after = jnp.where(i < last, cn_ref[0:1, :] * un_ref[0:1, :], 0.0)
    row = lax.broadcasted_iota(jnp.int32, z.shape, 0)
    z_m1 = jnp.where(row == 0, z_before, pltpu.roll(z, 1, axis=0))
    z_p1 = jnp.where(row == tm - 1, z_after, pltpu.roll(z, tm - 1, axis=0))
    w = w_ref[...]
    y = gb_ref[...] * (w[0:1, :] * z_m1 + w[1:2, :] * z + w[2:3, :] * z_p1)
    o_ref[...] = _group_rms_norm(y, g_ref[...]).astype(o_ref.dtype)


def _short_conv(pb, conv_w, g, batch, seq_len):
    tm = min(512, seq_len)
    per_tile = tm // SUBLANES
    n_row_blocks = seq_len // SUBLANES
    pb3 = pb.reshape(batch, seq_len, B_COLS)
    u_col, gb_col, gc_col = 0, 1, 2

    def main(col):
        return pl.BlockSpec((None, tm, GROUP_WIDTH), lambda b, i: (b, i, col))

    def prev(col):
        return pl.BlockSpec((None, SUBLANES, GROUP_WIDTH),
                            lambda b, i: (b, jnp.maximum(i * per_tile - 1, 0), col))

    def nxt(col):
        return pl.BlockSpec((None, SUBLANES, GROUP_WIDTH),
                            lambda b, i: (b, jnp.minimum((i + 1) * per_tile, n_row_blocks - 1), col))

    return pl.pallas_call(
        _short_conv_kernel,
        grid=(batch, seq_len // tm),
        in_specs=[main(u_col), main(gb_col), main(gc_col),
                  prev(u_col), nxt(u_col), prev(gc_col), nxt(gc_col),
                  _resident((CONV_WIDTH, GROUP_WIDTH), lambda b, i: (0, 0)),
                  _resident((1, GROUP_WIDTH), lambda b, i: (0, 0))],
        out_specs=pl.BlockSpec((None, tm, GROUP_WIDTH), lambda b, i: (b, i, 0)),
        out_shape=jax.ShapeDtypeStruct((batch, seq_len, GROUP_WIDTH), BF16),
        compiler_params=_params("parallel", "parallel"),
        name="short_conv",
    )(pb3, pb3, pb3, pb3, pb3, pb3, pb3, conv_w, g)


FOURIER_T2_PER_STEP = 8
FOURIER_K1_PER_STEP = SUBLANES


def _dft_cos_sin(n):
    idx = np.arange(n)
    ang = 2.0 * np.pi * ((idx[:, None] * idx[None, :]) % n) / n
    return np.cos(ang), np.sin(ang)


def _fourier_stage1_kernel(x_ref, wh_ref, wl_ref, tc_ref, ts_ref, o_ref):
    n1 = x_ref.shape[0]
    x_hi, x_lo = _split_bf16(x_ref[...])
    a = _dot3(wh_ref[...], wl_ref[...], x_hi, x_lo)
    a_re, a_im = a[:n1], a[n1:]
    tc, ts = tc_ref[...], ts_ref[...]
    o_ref[0] = a_re * tc + a_im * ts
    o_ref[1] = a_im * tc - a_re * ts


def _fourier_stage2_kernel(a_ref, w2h_ref, w2l_ref, w3h_ref, w3l_ref, g_ref, o_ref):
    n2 = a_ref.shape[2]
    for j in range(a_ref.shape[1]):
        a = jnp.concatenate([a_ref[0, j], a_ref[1, j]], axis=0)
        a_hi, a_lo = _split_bf16(a)
        y = _dot3(w2h_ref[...], w2l_ref[...], a_hi, a_lo)
        y2 = jnp.concatenate([y[:n2], y[n2:]], axis=1)
        y_hi, y_lo = _split_bf16(y2)
        z = _dot3(y_hi, y_lo, w3h_ref[...], w3l_ref[...])
        o_ref[:, j, :] = _group_rms_norm(z, g_ref[...]).astype(o_ref.dtype)


def _fourier(pc, g, batch, seq_len):
    n = math.isqrt(seq_len)
    assert n * n == seq_len and n % FOURIER_T2_PER_STEP == 0
    cols = FOURIER_T2_PER_STEP * GROUP_WIDTH

    cos_n, sin_n = _dft_cos_sin(n)
    w1_hi, w1_lo = _split_bf16(jnp.asarray(np.concatenate([cos_n, -sin_n], axis=0), F32))
    w2 = np.block([[cos_n, sin_n], [-sin_n, cos_n]])
    w2_hi, w2_lo = _split_bf16(jnp.asarray(w2, F32))
    cos_c, sin_c = _dft_cos_sin(HEAD_DIM)
    eye = np.eye(GROUP_WIDTH // HEAD_DIM)
    w3 = np.concatenate([np.kron(eye, cos_c), np.kron(eye, sin_c)], axis=0)
    w3_hi, w3_lo = _split_bf16(jnp.asarray(w3 / math.sqrt(seq_len * HEAD_DIM), F32))

    k1 = jnp.arange(n, dtype=jnp.int32)[:, None]
    t2 = jnp.arange(n, dtype=jnp.int32)[None, :]
    theta = (2.0 * math.pi / seq_len) * (k1 * t2).astype(F32)
    tw_cos = jnp.repeat(jnp.cos(theta), GROUP_WIDTH, axis=1)
    tw_sin = jnp.repeat(jnp.sin(theta), GROUP_WIDTH, axis=1)

    const = lambda shape: _resident(shape, lambda b, j: (0, 0))
    stage1 = pl.pallas_call(
        _fourier_stage1_kernel,
        grid=(batch, n // FOURIER_T2_PER_STEP),
        in_specs=[pl.BlockSpec((None, n, cols), lambda b, j: (b, 0, j)),
                  const((2 * n, n)), const((2 * n, n)),
                  pl.BlockSpec((n, cols), lambda b, j: (0, j)),
                  pl.BlockSpec((n, cols), lambda b, j: (0, j))],
        out_specs=pl.BlockSpec((None, 2, n, cols), lambda b, j: (b, 0, 0, j)),
        out_shape=jax.ShapeDtypeStruct((batch, 2, n, n * GROUP_WIDTH), F32),
        compiler_params=_params("parallel", "parallel"),
        name="fourier_stage1",
    )(pc.reshape(batch, n, n * GROUP_WIDTH), w1_hi, w1_lo, tw_cos, tw_sin)

    out = pl.pallas_call(
        _fourier_stage2_kernel,
        grid=(batch, n // FOURIER_K1_PER_STEP),
        in_specs=[pl.BlockSpec((None, 2, FOURIER_K1_PER_STEP, n, GROUP_WIDTH),
                               lambda b, j: (b, 0, j, 0, 0)),
                  const((2 * n, 2 * n)), const((2 * n, 2 * n)),
                  const((2 * GROUP_WIDTH, GROUP_WIDTH)), const((2 * GROUP_WIDTH, GROUP_WIDTH)),
                  const((1, GROUP_WIDTH))],
        out_specs=pl.BlockSpec((None, n, FOURIER_K1_PER_STEP, GROUP_WIDTH),
                               lambda b, j: (b, 0, j, 0)),
        out_shape=jax.ShapeDtypeStruct((batch, n, n, GROUP_WIDTH), F32),
        compiler_params=_params("parallel", "parallel"),
        name="fourier_stage2",
    )(stage1.reshape(batch, 2, n, n, GROUP_WIDTH), w2_hi, w2_lo, w3_hi, w3_lo, g)
    return out.reshape(batch, seq_len, GROUP_WIDTH)


def _rope_tables(seq_len):
    pos = jnp.arange(seq_len)
    inv_freq = ROPE_THETA ** (-jnp.arange(0, AXIS_DIM, 2, dtype=F32) / AXIS_DIM)
    ang_row = (pos // GRID_W).astype(F32)[:, None] * inv_freq[None, :]
    ang_col = (pos % GRID_W).astype(F32)[:, None] * inv_freq[None, :]
    ang = jnp.concatenate([ang_row, ang_row, ang_col, ang_col], axis=-1)
    half = AXIS_DIM // 2
    sign = np.tile(np.concatenate([-np.ones(half), np.ones(half)]), HEAD_DIM // AXIS_DIM)
    cos = jnp.tile(jnp.cos(ang), (1, LANES // HEAD_DIM))
    sin_signed = jnp.tile(jnp.sin(ang) * jnp.asarray(sign, F32), (1, LANES // HEAD_DIM))
    return cos, sin_signed


def _head_norm_rope(x, gain, cos, sin_signed, head_mean):
    width = x.shape[-1]
    sq_hi, sq_lo = _split_bf16(x * x)
    ms = _dot(sq_hi, head_mean) + _dot(sq_lo, head_mean)
    xn = x * lax.rsqrt(ms + RMS_EPS) * gain
    reps = width // LANES
    if reps > 1:
        cos = jnp.concatenate([cos] * reps, axis=-1)
        sin_signed = jnp.concatenate([sin_signed] * reps, axis=-1)
    half = AXIS_DIM // 2
    lane = lax.broadcasted_iota(jnp.int32, x.shape, 1)
    first_half = (lane % AXIS_DIM) < half
    partner = jnp.where(first_half, pltpu.roll(xn, width - half, axis=1),
                        pltpu.roll(xn, half, axis=1))
    return xn * cos + partner * sin_signed


def _dense_prep_kernel(q_ref, k_ref, v_ref, cos_ref, sin_ref, gq_ref, gk_ref, hm_ref,
                       qt_ref, ko_ref, vt_ref):
    cos, sin_signed = cos_ref[...], sin_ref[...]
    hm = hm_ref[...]
    q = _head_norm_rope(q_ref[...], gq_ref[...], cos, sin_signed, hm) * (HEAD_DIM ** -0.5)
    k = _head_norm_rope(k_ref[...], gk_ref[...], cos, sin_signed, hm[:KV_WIDTH, :KV_WIDTH])
    qt_ref[...] = q.T.astype(BF16)
    ko_ref[...] = k.astype(BF16)
    vt_ref[...] = v_ref[...].T.astype(BF16)


def _dense_prep(pd, qn_g, kn_g, batch, seq_len):
    tm = min(512, seq_len)
    pd3 = pd.reshape(batch, seq_len, D_COLS)
    cos, sin_signed = _rope_tables(seq_len)
    gq = jnp.tile(qn_g, GROUP_WIDTH // HEAD_DIM).reshape(1, GROUP_WIDTH)
    gk = jnp.tile(kn_g, KV_WIDTH // HEAD_DIM).reshape(1, KV_WIDTH)
    head_mean = jnp.asarray(
        np.kron(np.eye(GROUP_WIDTH // HEAD_DIM), np.full((HEAD_DIM, HEAD_DIM), 1.0 / HEAD_DIM)),
        BF16)
    k_col, v_col = GROUP_WIDTH // KV_WIDTH, GROUP_WIDTH // KV_WIDTH + 1
    const = lambda shape: _resident(shape, lambda b, i: (0, 0))
    return pl.pallas_call(
        _dense_prep_kernel,
        grid=(batch, seq_len // tm),
        in_specs=[pl.BlockSpec((None, tm, GROUP_WIDTH), lambda b, i: (b, i, 0)),
                  pl.BlockSpec((None, tm, KV_WIDTH), lambda b, i: (b, i, k_col)),
                  pl.BlockSpec((None, tm, KV_WIDTH), lambda b, i: (b, i, v_col)),
                  pl.BlockSpec((tm, LANES), lambda b, i: (i, 0)),
                  pl.BlockSpec((tm, LANES), lambda b, i: (i, 0)),
                  const((1, GROUP_WIDTH)), const((1, KV_WIDTH)),
                  const((GROUP_WIDTH, GROUP_WIDTH))],
        out_specs=[pl.BlockSpec((None, GROUP_WIDTH, tm), lambda b, i: (b, 0, i)),
                   pl.BlockSpec((None, tm, KV_WIDTH), lambda b, i: (b, i, 0)),
                   pl.BlockSpec((None, KV_WIDTH, tm), lambda b, i: (b, 0, i))],
        out_shape=[jax.ShapeDtypeStruct((batch, GROUP_WIDTH, seq_len), BF16),
                   jax.ShapeDtypeStruct((batch, seq_len, KV_WIDTH), BF16),
                   jax.ShapeDtypeStruct((batch, KV_WIDTH, seq_len), BF16)],
        compiler_params=_params("parallel", "parallel"),
        name="dense_prep",
    )(pd3, pd3, pd3, cos, sin_signed, gq, gk, head_mean)


def _dense_attn_kernel(qt_ref, k_ref, vt_ref, g_ref, o_ref, qpad_ref, m_ref, l_ref, acc_ref):
    j = pl.program_id(2)
    tq = qt_ref.shape[1]

    @pl.when(j == 0)
    def _():
        m_ref[...] = jnp.full(m_ref.shape, -jnp.inf, F32)
        l_ref[...] = jnp.zeros(l_ref.shape, F32)
        acc_ref[...] = jnp.zeros(acc_ref.shape, F32)
        zeros = jnp.zeros((HEAD_DIM, tq), BF16)
        for h in range(Q_HEADS):
            qh = qt_ref[h * HEAD_DIM:(h + 1) * HEAD_DIM, :]
            qpad_ref[h] = jnp.concatenate([qh, zeros] if h < Q_HEADS // KV_HEADS else [zeros, qh],
                                          axis=0)

    k = k_ref[...]
    for h in range(Q_HEADS):
        kv = h // (Q_HEADS // KV_HEADS)
        s = _dot(k, qpad_ref[h])
        m_old = m_ref[h:h + 1, :]
        m_new = jnp.maximum(m_old, jnp.max(s, axis=0, keepdims=True))
        p = jnp.exp(s - m_new)
        alpha = jnp.exp(m_old - m_new)
        l_ref[h:h + 1, :] = alpha * l_ref[h:h + 1, :] + jnp.sum(p, axis=0, keepdims=True)
        pv = _dot(vt_ref[kv * HEAD_DIM:(kv + 1) * HEAD_DIM, :], p.astype(BF16))
        rows = slice(h * HEAD_DIM, (h + 1) * HEAD_DIM)
        acc_ref[rows, :] = alpha * acc_ref[rows, :] + pv
        m_ref[h:h + 1, :] = m_new

    @pl.when(j == pl.num_programs(2) - 1)
    def _():
        parts = [acc_ref[h * HEAD_DIM:(h + 1) * HEAD_DIM, :] / l_ref[h:h + 1, :]
                 for h in range(Q_HEADS)]
        out = jnp.concatenate(parts, axis=0).T
        o_ref[...] = _group_rms_norm(out, g_ref[...]).astype(o_ref.dtype)


def _dense_attn(qt, k, vt, g, batch, seq_len):
    tq = min(512, seq_len)
    tk = min(512, seq_len)
    return pl.pallas_call(
        _dense_attn_kernel,
        grid=(batch, seq_len // tq, seq_len // tk),
        in_specs=[pl.BlockSpec((None, GROUP_WIDTH, tq), lambda b, i, j: (b, 0, i)),
                  pl.BlockSpec((None, tk, KV_WIDTH), lambda b, i, j: (b, j, 0)),
                  pl.BlockSpec((None, KV_WIDTH, tk), lambda b, i, j: (b, 0, j)),
                  _resident((1, GROUP_WIDTH), lambda b, i, j: (0, 0))],
        out_specs=pl.BlockSpec((None, tq, GROUP_WIDTH), lambda b, i, j: (b, i, 0)),
        out_shape=jax.ShapeDtypeStruct((batch, seq_len, GROUP_WIDTH), BF16),
        scratch_shapes=[pltpu.VMEM((Q_HEADS, KV_WIDTH, tq), BF16),
                        pltpu.VMEM((Q_HEADS, tq), F32),
                        pltpu.VMEM((Q_HEADS, tq), F32),
                        pltpu.VMEM((GROUP_WIDTH, tq), F32)],
        compiler_params=_params("parallel", "parallel", "arbitrary"),
        name="dense_attn",
    )(qt, k, vt, g)


FF_CHUNK = 1024


def _out_mlp_kernel(ma_ref, mb_ref, mc_ref, md_ref, h_ref, wo_ref, g1_ref, b1_ref,
                    w1_ref, w2_ref, g2_ref, b2_ref, o_ref):
    mix = jnp.concatenate([r[...].astype(BF16) for r in (ma_ref, mb_ref, mc_ref, md_ref)], axis=-1)
    h1 = _layer_norm(DN_ALPHA * h_ref[...] + _dot(mix, wo_ref[...]), g1_ref[...], b1_ref[...])
    h1_bf = h1.astype(BF16)
    ffn = jnp.zeros(h1.shape, F32)
    for c in range(D_FF // FF_CHUNK):
        cols = slice(c * FF_CHUNK, (c + 1) * FF_CHUNK)
        u = jnp.maximum(_dot(h1_bf, w1_ref[:, cols]), 0.0)
        ffn = ffn + _dot((u * u).astype(BF16), w2_ref[cols, :])
    o_ref[...] = _layer_norm(DN_ALPHA * h1 + ffn, g2_ref[...], b2_ref[...])


def _out_mlp(mixes, h, w_out, g1, b1, w1, w2, g2, b2):
    rows = h.shape[0]
    tm = min(512, rows)
    row_spec = lambda width: pl.BlockSpec((tm, width), lambda i: (i, 0))
    const = lambda shape: _resident(shape, lambda i: (0, 0))
    vec = const((1, D_MODEL))
    return pl.pallas_call(
        _out_mlp_kernel,
        grid=(rows // tm,),
        in_specs=[row_spec(GROUP_WIDTH)] * 4 + [row_spec(D_MODEL), const((D_MODEL, D_MODEL)),
                                                 vec, vec, const((D_MODEL, D_FF)),
                                                 const((D_FF, D_MODEL)), vec, vec],
        out_specs=row_spec(D_MODEL),
        out_shape=jax.ShapeDtypeStruct((rows, D_MODEL), F32),
        compiler_params=_params("parallel"),
        name="out_mlp",
    )(*mixes, h, w_out, g1, b1, w1, w2, g2, b2)


def kernel(x, ln_in_g, ln_in_b, w_in, conv_w, sink, qn_g, kn_g, grp_g, w_out, ln1_g, ln1_b,
           w1, w2, ln2_g, ln2_b):
    batch, seq_len, _ = x.shape
    rows = batch * seq_len
    depth = w_in.shape[0]
    row_vec = lambda v: v.reshape(1, -1)

    h = x.reshape(rows, D_MODEL)
    for l in range(depth):
        first = l == 0
        outs = _in_proj(h, row_vec(ln_in_g), row_vec(ln_in_b), w_in[l].astype(BF16), first)
        if first:
            h, pa, pb, pc, pd = outs
        else:
            pa, pb, pc, pd = outs
        g = [row_vec(grp_g[l, i * GROUP_WIDTH:(i + 1) * GROUP_WIDTH]) for i in range(4)]
        mix_a = _window_attn(pa, sink[l], g[0], batch, seq_len)
        mix_b = _short_conv(pb, conv_w[l], g[1], batch, seq_len)
        mix_c = _fourier(pc, g[2], batch, seq_len)
        qt, kd, vt = _dense_prep(pd, qn_g[l], kn_g[l], batch, seq_len)
        mix_d = _dense_attn(qt, kd, vt, g[3], batch, seq_len)
        mixes = [m.reshape(rows, GROUP_WIDTH) for m in (mix_a, mix_b, mix_c, mix_d)]
        h = _out_mlp(mixes, h, w_out[l].astype(BF16), row_vec(ln1_g[l]), row_vec(ln1_b[l]),
                     w1[l].astype(BF16), w2[l].astype(BF16), row_vec(ln2_g[l]), row_vec(ln2_b[l]))
    return h.reshape(batch, seq_len, D_MODEL)
```

```python
import functools
import math

import numpy as np
import jax
import jax.numpy as jnp
from jax import lax
from jax.experimental import pallas as pl
from jax.experimental.pallas import tpu as pltpu

F32 = jnp.float32
BF16 = jnp.bfloat16

D_MODEL = 1024
HEAD_DIM = 64
GROUP_WIDTH = 256
Q_HEADS = 4
KV_HEADS = 2
KV_WIDTH = KV_HEADS * HEAD_DIM
CONV_WIDTH = 3
WINDOW = 128
BLOCK = 128
GRID_W = 64
AXIS_DIM = HEAD_DIM // 2
ROPE_THETA = 10000.0
D_FF = 4 * D_MODEL
LN_EPS = 1e-5
RMS_EPS = 1e-6
DEPTH = 2
DN_ALPHA = (2 * DEPTH) ** 0.25
NEG_INF = -1e30
LOG2_E = math.log2(math.e)
ALIBI_SLOPES = tuple(2.0 ** (-8.0 * (h + 1) / Q_HEADS) for h in range(Q_HEADS))

V7X_VMEM_LIMIT_BYTES = 56 * 1024 * 1024
LANES = 128
SUBLANES = 8


def _params(*semantics):
    return pltpu.CompilerParams(dimension_semantics=semantics,
                                vmem_limit_bytes=V7X_VMEM_LIMIT_BYTES)


def _resident(block_shape, index_map):
    return pl.BlockSpec(block_shape, index_map, pipeline_mode=pl.Buffered(1))


def _dot(a, b):
    return jnp.dot(a, b, preferred_element_type=F32)


def _split_bf16(x):
    hi = x.astype(BF16)
    lo = (x - hi.astype(F32)).astype(BF16)
    return hi, lo


def _dot3(a_hi, a_lo, b_hi, b_lo):
    return _dot(a_hi, b_hi) + _dot(a_hi, b_lo) + _dot(a_lo, b_hi)


def _layer_norm(x, g, b):
    mu = jnp.mean(x, axis=-1, keepdims=True)
    xc = x - mu
    var = jnp.mean(xc * xc, axis=-1, keepdims=True)
    return xc * lax.rsqrt(var + LN_EPS) * g + b


def _group_rms_norm(x, g):
    ms = jnp.mean(x * x, axis=-1, keepdims=True)
    return x * lax.rsqrt(ms + RMS_EPS) * g


A_COLS = 512
B_COLS = 768
C_COLS = 256
D_COLS = 512
D_IN_PROJ = A_COLS + B_COLS + C_COLS + D_COLS


def _in_proj_kernel(apply_ln, x_ref, g_ref, b_ref, w_ref, *out_refs):
    x = x_ref[...]
    if apply_ln:
        h_ref, pa_ref, pb_ref, pc_ref, pd_ref = out_refs
        x = _layer_norm(x, g_ref[...], b_ref[...])
        h_ref[...] = x
    else:
        pa_ref, pb_ref, pc_ref, pd_ref = out_refs
    proj = _dot(x.astype(BF16), w_ref[...])
    pa_ref[...] = proj[:, :A_COLS]
    pb_ref[...] = proj[:, A_COLS:A_COLS + B_COLS]
    pc_ref[...] = proj[:, A_COLS + B_COLS:A_COLS + B_COLS + C_COLS]
    pd_ref[...] = proj[:, A_COLS + B_COLS + C_COLS:]


def _in_proj(x, g, b, w, apply_ln):
    rows = x.shape[0]
    tm = min(512, rows)
    row_spec = lambda width: pl.BlockSpec((tm, width), lambda i: (i, 0))
    const = lambda shape: _resident(shape, lambda i: (0, 0))
    widths = ([D_MODEL] if apply_ln else []) + [A_COLS, B_COLS, C_COLS, D_COLS]
    return pl.pallas_call(
        functools.partial(_in_proj_kernel, apply_ln),
        grid=(rows // tm,),
        in_specs=[row_spec(D_MODEL), const((1, D_MODEL)), const((1, D_MODEL)),
                  const((D_MODEL, D_IN_PROJ))],
        out_specs=[row_spec(wd) for wd in widths],
        out_shape=[jax.ShapeDtypeStruct((rows, wd), F32) for wd in widths],
        compiler_params=_params("parallel"),
        name="in_proj_ln" if apply_ln else "in_proj",
    )(x, g, b, w)


def _window_attn_kernel(seq_len, q_ref, km_ref, kp_ref, kn_ref, vm_ref, vp_ref, vn_ref,
                        sink_ref, g_ref, o_ref, kall_ref, vall_ref):
    tq = q_ref.shape[0]
    n_blk = tq // BLOCK
    tile_start = pl.program_id(1) * tq

    kall_ref[0:BLOCK, :] = kp_ref[...].astype(BF16)
    kall_ref[BLOCK:BLOCK + tq, :] = km_ref[...].astype(BF16)
    kall_ref[BLOCK + tq:, :] = kn_ref[...].astype(BF16)
    vall_ref[0:BLOCK, :] = vp_ref[...].astype(BF16)
    vall_ref[BLOCK:BLOCK + tq, :] = vm_ref[...].astype(BF16)
    vall_ref[BLOCK + tq:, :] = vn_ref[...].astype(BF16)

    r_idx = lax.broadcasted_iota(jnp.int32, (BLOCK, 3 * BLOCK), 0)
    c_idx = lax.broadcasted_iota(jnp.int32, (BLOCK, 3 * BLOCK), 1)
    dist = jnp.abs(BLOCK + r_idx - c_idx)
    in_window = dist <= WINDOW
    dist_f = dist.astype(F32)
    lane = lax.broadcasted_iota(jnp.int32, (BLOCK, LANES), 1)
    low_half = lane < HEAD_DIM
    g = g_ref[...]

    for jb in range(n_blk):
        kw = kall_ref[jb * BLOCK:(jb + 3) * BLOCK, :]
        vw = vall_ref[jb * BLOCK:(jb + 3) * BLOCK, :]
        key_pos = tile_start + (jb - 1) * BLOCK + c_idx
        valid = in_window & (key_pos >= 0) & (key_pos < seq_len)
        q_blk = q_ref[jb * BLOCK:(jb + 1) * BLOCK, :] * (HEAD_DIM ** -0.5)
        q_lo, q_hi = q_blk[:, :LANES], q_blk[:, LANES:]
        q_heads = (
            jnp.where(low_half, q_lo, 0.0),
            jnp.where(low_half, pltpu.roll(q_lo, HEAD_DIM, axis=1), 0.0),
            jnp.where(low_half, 0.0, pltpu.roll(q_hi, HEAD_DIM, axis=1)),
            jnp.where(low_half, 0.0, q_hi),
        )
        outs = []
        for h in range(Q_HEADS):
            s = lax.dot_general(q_heads[h].astype(BF16), kw, (((1,), (1,)), ((), ())),
                                preferred_element_type=F32)
            s = jnp.where(valid, s - ALIBI_SLOPES[h] * dist_f, NEG_INF)
            sink = sink_ref[h]
            m = jnp.maximum(jnp.max(s, axis=-1, keepdims=True), sink)
            e = jnp.exp(s - m)
            den = jnp.sum(e, axis=-1, keepdims=True) + jnp.exp(sink - m)
            outs.append(_dot(e.astype(BF16), vw) / den)
        out_lo = jnp.where(low_half, outs[0], pltpu.roll(outs[1], HEAD_DIM, axis=1))
        out_hi = jnp.where(low_half, pltpu.roll(outs[2], HEAD_DIM, axis=1), outs[3])
        out = jnp.concatenate([out_lo, out_hi], axis=-1)
        o_ref[jb * BLOCK:(jb + 1) * BLOCK, :] = _group_rms_norm(out, g).astype(o_ref.dtype)


def _window_attn(pa, sink, g, batch, seq_len):
    tq = min(512, seq_len)
    per_tile = tq // BLOCK
    n_blocks = seq_len // BLOCK
    pa3 = pa.reshape(batch, seq_len, A_COLS)
    k_col, v_col = GROUP_WIDTH // KV_WIDTH, GROUP_WIDTH // KV_WIDTH + 1

    def main(col):
        return pl.BlockSpec((None, tq, KV_WIDTH), lambda b, i: (b, i, col))

    def prev(col):
        return pl.BlockSpec((None, BLOCK, KV_WIDTH),
                            lambda b, i: (b, jnp.maximum(i * per_tile - 1, 0), col))

    def nxt(col):
        return pl.BlockSpec((None, BLOCK, KV_WIDTH),
                            lambda b, i: (b, jnp.minimum((i + 1) * per_tile, n_blocks - 1), col))

    return pl.pallas_call(
        functools.partial(_window_attn_kernel, seq_len),
        grid=(batch, seq_len // tq),
        in_specs=[pl.BlockSpec((None, tq, GROUP_WIDTH), lambda b, i: (b, i, 0)),
                  main(k_col), prev(k_col), nxt(k_col),
                  main(v_col), prev(v_col), nxt(v_col),
                  pl.BlockSpec(memory_space=pltpu.SMEM),
                  _resident((1, GROUP_WIDTH), lambda b, i: (0, 0))],
        out_specs=pl.BlockSpec((None, tq, GROUP_WIDTH), lambda b, i: (b, i, 0)),
        out_shape=jax.ShapeDtypeStruct((batch, seq_len, GROUP_WIDTH), BF16),
        scratch_shapes=[pltpu.VMEM((tq + 2 * BLOCK, KV_WIDTH), BF16),
                        pltpu.VMEM((tq + 2 * BLOCK, KV_WIDTH), BF16)],
        compiler_params=_params("parallel", "parallel"),
        name="window_attn",
    )(pa3, pa3, pa3, pa3, pa3, pa3, pa3, sink, g)


def _short_conv_kernel(u_ref, gb_ref, gc_ref, up_ref, un_ref, cp_ref, cn_ref, w_ref, g_ref,
                       o_ref):
    tm = u_ref.shape[0]
    i = pl.program_id(1)
    last = pl.num_programs(1) - 1
    z = gc_ref[...] * u_ref[...]
    z_before = jnp.where(i > 0, cp_ref[SUBLANES - 1:SUBLANES, :] * up_ref[SUBLANES - 1:SUBLANES, :],
                         0.0)
    z_after = jnp.where(i < last, cn_ref[0:1, :] * un_ref[0:1, :], 0.0)
    row = lax.broadcasted_iota(jnp.int32, z.shape, 0)
    z_m1 = jnp.where(row == 0, z_before, pltpu.roll(z, 1, axis=0))
    z_p1 = jnp.where(row == tm - 1, z_after, pltpu.roll(z, tm - 1, axis=0))
    w = w_ref[...]
    y = gb_ref[...] * (w[0:1, :] * z_m1 + w[1:2, :] * z + w[2:3, :] * z_p1)
    o_ref[...] = _group_rms_norm(y, g_ref[...]).astype(o_ref.dtype)


def _short_conv(pb, conv_w, g, batch, seq_len):
    tm = min(512, seq_len)
    per_tile = tm // SUBLANES
    n_row_blocks = seq_len // SUBLANES
    pb3 = pb.reshape(batch, seq_len, B_COLS)
    u_col, gb_col, gc_col = 0, 1, 2

    def main(col):
        return pl.BlockSpec((None, tm, GROUP_WIDTH), lambda b, i: (b, i, col))

    def prev(col):
        return pl.BlockSpec((None, SUBLANES, GROUP_WIDTH),
                            lambda b, i: (b, jnp.maximum(i * per_tile - 1, 0), col))

    def nxt(col):
        return pl.BlockSpec((None, SUBLANES, GROUP_WIDTH),
                            lambda b, i: (b, jnp.minimum((i + 1) * per_tile, n_row_blocks - 1), col))

    return pl.pallas_call(
        _short_conv_kernel,
        grid=(batch, seq_len // tm),
        in_specs=[main(u_col), main(gb_col), main(gc_col),
                  prev(u_col), nxt(u_col), prev(gc_col), nxt(gc_col),
                  _resident((CONV_WIDTH, GROUP_WIDTH), lambda b, i: (0, 0)),
                  _resident((1, GROUP_WIDTH), lambda b, i: (0, 0))],
        out_specs=pl.BlockSpec((None, tm, GROUP_WIDTH), lambda b, i: (b, i, 0)),
        out_shape=jax.ShapeDtypeStruct((batch, seq_len, GROUP_WIDTH), BF16),
        compiler_params=_params("parallel", "parallel"),
        name="short_conv",
    )(pb3, pb3, pb3, pb3, pb3, pb3, pb3, conv_w, g)


FOURIER_T2_PER_STEP = 8
FOURIER_K1_PER_STEP = SUBLANES


def _dft_cos_sin(n):
    idx = np.arange(n)
    ang = 2.0 * np.pi * ((idx[:, None] * idx[None, :]) % n) / n
    return np.cos(ang), np.sin(ang)


def _fourier_stage1_kernel(x_ref, wh_ref, wl_ref, tc_ref, ts_ref, o_ref):
    n1 = x_ref.shape[0]
    x_hi, x_lo = _split_bf16(x_ref[...])
    a = _dot3(wh_ref[...], wl_ref[...], x_hi, x_lo)
    a_re, a_im = a[:n1], a[n1:]
    tc, ts = tc_ref[...], ts_ref[...]
    o_ref[0] = a_re * tc + a_im * ts
    o_ref[1] = a_im * tc - a_re * ts


def _fourier_stage2_kernel(a_ref, w2h_ref, w2l_ref, w3h_ref, w3l_ref, g_ref, o_ref):
    n2 = a_ref.shape[2]
    for j in range(a_ref.shape[1]):
        a = jnp.concatenate([a_ref[0, j], a_ref[1, j]], axis=0)
        a_hi, a_lo = _split_bf16(a)
        y = _dot3(w2h_ref[...], w2l_ref[...], a_hi, a_lo)
        y2 = jnp.concatenate([y[:n2], y[n2:]], axis=1)
        y_hi, y_lo = _split_bf16(y2)
        z = _dot3(y_hi, y_lo, w3h_ref[...], w3l_ref[...])
        o_ref[:, j, :] = _group_rms_norm(z, g_ref[...]).astype(o_ref.dtype)


def _fourier(pc, g, batch, seq_len):
    n = math.isqrt(seq_len)
    assert n * n == seq_len and n % FOURIER_T2_PER_STEP == 0
    cols = FOURIER_T2_PER_STEP * GROUP_WIDTH

    cos_n, sin_n = _dft_cos_sin(n)
    w1_hi, w1_lo = _split_bf16(jnp.asarray(np.concatenate([cos_n, -sin_n], axis=0), F32))
    w2 = np.block([[cos_n, sin_n], [-sin_n, cos_n]])
    w2_hi, w2_lo = _split_bf16(jnp.asarray(w2, F32))
    cos_c, sin_c = _dft_cos_sin(HEAD_DIM)
    eye = np.eye(GROUP_WIDTH // HEAD_DIM)
    w3 = np.concatenate([np.kron(eye, cos_c), np.kron(eye, sin_c)], axis=0)
    w3_hi, w3_lo = _split_bf16(jnp.asarray(w3 / math.sqrt(seq_len * HEAD_DIM), F32))

    k1 = jnp.arange(n, dtype=jnp.int32)[:, None]
    t2 = jnp.arange(n, dtype=jnp.int32)[None, :]
    theta = (2.0 * math.pi / seq_len) * (k1 * t2).astype(F32)
    tw_cos = jnp.repeat(jnp.cos(theta), GROUP_WIDTH, axis=1)
    tw_sin = jnp.repeat(jnp.sin(theta), GROUP_WIDTH, axis=1)

    const = lambda shape: _resident(shape, lambda b, j: (0, 0))
    stage1 = pl.pallas_call(
        _fourier_stage1_kernel,
        grid=(batch, n // FOURIER_T2_PER_STEP),
        in_specs=[pl.BlockSpec((None, n, cols), lambda b, j: (b, 0, j)),
                  const((2 * n, n)), const((2 * n, n)),
                  pl.BlockSpec((n, cols), lambda b, j: (0, j)),
                  pl.BlockSpec((n, cols), lambda b, j: (0, j))],
        out_specs=pl.BlockSpec((None, 2, n, cols), lambda b, j: (b, 0, 0, j)),
        out_shape=jax.ShapeDtypeStruct((batch, 2, n, n * GROUP_WIDTH), F32),
        compiler_params=_params("parallel", "parallel"),
        name="fourier_stage1",
    )(pc.reshape(batch, n, n * GROUP_WIDTH), w1_hi, w1_lo, tw_cos, tw_sin)

    out = pl.pallas_call(
        _fourier_stage2_kernel,
        grid=(batch, n // FOURIER_K1_PER_STEP),
        in_specs=[pl.BlockSpec((None, 2, FOURIER_K1_PER_STEP, n, GROUP_WIDTH),
                               lambda b, j: (b, 0, j, 0, 0)),
                  const((2 * n, 2 * n)), const((2 * n, 2 * n)),
                  const((2 * GROUP_WIDTH, GROUP_WIDTH)), const((2 * GROUP_WIDTH, GROUP_WIDTH)),
                  const((1, GROUP_WIDTH))],
        out_specs=pl.BlockSpec((None, n, FOURIER_K1_PER_STEP, GROUP_WIDTH),
                               lambda b, j: (b, 0, j, 0)),
        out_shape=jax.ShapeDtypeStruct((batch, n, n, GROUP_WIDTH), F32),
        compiler_params=_params("parallel", "parallel"),
        name="fourier_stage2",
    )(stage1.reshape(batch, 2, n, n, GROUP_WIDTH), w2_hi, w2_lo, w3_hi, w3_lo, g)
    return out.reshape(batch, seq_len, GROUP_WIDTH)


def _rope_tables(seq_len):
    pos = jnp.arange(seq_len)
    inv_freq = ROPE_THETA ** (-jnp.arange(0, AXIS_DIM, 2, dtype=F32) / AXIS_DIM)
    ang_row = (pos // GRID_W).astype(F32)[:, None] * inv_freq[None, :]
    ang_col = (pos % GRID_W).astype(F32)[:, None] * inv_freq[None, :]
    ang = jnp.concatenate([ang_row, ang_row, ang_col, ang_col], axis=-1)
    half = AXIS_DIM // 2
    sign = np.tile(np.concatenate([-np.ones(half), np.ones(half)]), HEAD_DIM // AXIS_DIM)
    cos = jnp.tile(jnp.cos(ang), (1, LANES // HEAD_DIM))
    sin_signed = jnp.tile(jnp.sin(ang) * jnp.asarray(sign, F32), (1, LANES // HEAD_DIM))
    return cos, sin_signed


def _head_norm_rope(x, gain, cos, sin_signed, head_mean):
    width = x.shape[-1]
    sq_hi, sq_lo = _split_bf16(x * x)
    ms = _dot(sq_hi, head_mean) + _dot(sq_lo, head_mean)
    xn = x * lax.rsqrt(ms + RMS_EPS) * gain
    reps = width // LANES
    if reps > 1:
        cos = jnp.concatenate([cos] * reps, axis=-1)
        sin_signed = jnp.concatenate([sin_signed] * reps, axis=-1)
    half = AXIS_DIM // 2
    lane = lax.broadcasted_iota(jnp.int32, x.shape, 1)
    first_half = (lane % AXIS_DIM) < half
    partner = jnp.where(first_half, pltpu.roll(xn, width - half, axis=1),
                        pltpu.roll(xn, half, axis=1))
    return xn * cos + partner * sin_signed


def _dense_prep_kernel(q_ref, k_ref, v_ref, cos_ref, sin_ref, gq_ref, gk_ref, hm_ref,
                       qt_ref, ko_ref, vt_ref):
    cos, sin_signed = cos_ref[...], sin_ref[...]
    hm = hm_ref[...]
    q = _head_norm_rope(q_ref[...], gq_ref[...], cos, sin_signed, hm) * (HEAD_DIM ** -0.5 * LOG2_E)
    k = _head_norm_rope(k_ref[...], gk_ref[...], cos, sin_signed, hm[:KV_WIDTH, :KV_WIDTH])
    qt_ref[...] = q.T.astype(BF16)
    ko_ref[...] = k.astype(BF16)
    vt_ref[...] = v_ref[...].T.astype(BF16)


def _dense_prep(pd, qn_g, kn_g, batch, seq_len):
    tm = min(512, seq_len)
    pd3 = pd.reshape(batch, seq_len, D_COLS)
    cos, sin_signed = _rope_tables(seq_len)
    gq = jnp.tile(qn_g, GROUP_WIDTH // HEAD_DIM).reshape(1, GROUP_WIDTH)
    gk = jnp.tile(kn_g, KV_WIDTH // HEAD_DIM).reshape(1, KV_WIDTH)
    head_mean = jnp.asarray(
        np.kron(np.eye(GROUP_WIDTH // HEAD_DIM), np.full((HEAD_DIM, HEAD_DIM), 1.0 / HEAD_DIM)),
        BF16)
    k_col, v_col = GROUP_WIDTH // KV_WIDTH, GROUP_WIDTH // KV_WIDTH + 1
    const = lambda shape: _resident(shape, lambda b, i: (0, 0))
    return pl.pallas_call(
        _dense_prep_kernel,
        grid=(batch, seq_len // tm),
        in_specs=[pl.BlockSpec((None, tm, GROUP_WIDTH), lambda b, i: (b, i, 0)),
                  pl.BlockSpec((None, tm, KV_WIDTH), lambda b, i: (b, i, k_col)),
                  pl.BlockSpec((None, tm, KV_WIDTH), lambda b, i: (b, i, v_col)),
                  pl.BlockSpec((tm, LANES), lambda b, i: (i, 0)),
                  pl.BlockSpec((tm, LANES), lambda b, i: (i, 0)),
                  const((1, GROUP_WIDTH)), const((1, KV_WIDTH)),
                  const((GROUP_WIDTH, GROUP_WIDTH))],
        out_specs=[pl.BlockSpec((None, GROUP_WIDTH, tm), lambda b, i: (b, 0, i)),
                   pl.BlockSpec((None, tm, KV_WIDTH), lambda b, i: (b, i, 0)),
                   pl.BlockSpec((None, KV_WIDTH, tm), lambda b, i: (b, 0, i))],
        out_shape=[jax.ShapeDtypeStruct((batch, GROUP_WIDTH, seq_len), BF16),
                   jax.ShapeDtypeStruct((batch, seq_len, KV_WIDTH), BF16),
                   jax.ShapeDtypeStruct((batch, KV_WIDTH, seq_len), BF16)],
        compiler_params=_params("parallel", "parallel"),
        name="dense_prep",
    )(pd3, pd3, pd3, cos, sin_signed, gq, gk, head_mean)


DENSE_TQ = 512
DENSE_TK = 512


def _dense_attn_kernel(tk, qt_ref, k_ref, vt_ref, g_ref, o_ref, qpad_ref, s_ref, m_ref, l_ref,
                       acc_ref):
    tq = qt_ref.shape[1]
    n_chunks = k_ref.shape[0] // tk
    group = Q_HEADS // KV_HEADS

    m_ref[...] = jnp.full(m_ref.shape, -jnp.inf, F32)
    l_ref[...] = jnp.zeros(l_ref.shape, F32)
    acc_ref[...] = jnp.zeros(acc_ref.shape, F32)
    zeros = jnp.zeros((HEAD_DIM, tq), BF16)
    for h in range(Q_HEADS):
        qh = qt_ref[h * HEAD_DIM:(h + 1) * HEAD_DIM, :]
        qpad_ref[h] = jnp.concatenate([qh, zeros] if h < group else [zeros, qh], axis=0)

    def scores(j, h):
        start = pl.multiple_of(j * tk, tk)
        s = _dot(k_ref[pl.ds(start, tk), :], qpad_ref[h])
        s_ref[h % 2] = s
        return jnp.max(s, axis=0, keepdims=True)

    def softmax_pv(j, h, s_max):
        start = pl.multiple_of(j * tk, tk)
        m_old = m_ref[h:h + 1, :]
        m_new = jnp.maximum(m_old, s_max)
        p = jnp.exp2(s_ref[h % 2] - m_new)
        alpha = jnp.exp2(m_old - m_new)
        l_ref[h:h + 1, :] = alpha * l_ref[h:h + 1, :] + jnp.sum(p, axis=0, keepdims=True)
        kv = h // group
        vt = vt_ref[kv * HEAD_DIM:(kv + 1) * HEAD_DIM, pl.ds(start, tk)]
        rows = slice(h * HEAD_DIM, (h + 1) * HEAD_DIM)
        acc_ref[rows, :] = alpha * acc_ref[rows, :] + _dot(vt, p.astype(BF16))
        m_ref[h:h + 1, :] = m_new

    def chunk(j, s_max):
        for h in range(Q_HEADS):
            if h + 1 < Q_HEADS:
                nxt = scores(j, h + 1)
            else:
                nxt = scores(jnp.minimum(j + 1, n_chunks - 1), 0)
            softmax_pv(j, h, s_max)
            s_max = nxt
        return s_max

    lax.fori_loop(0, n_chunks, chunk, scores(0, 0), unroll=2)

    parts = [acc_ref[h * HEAD_DIM:(h + 1) * HEAD_DIM, :] / l_ref[h:h + 1, :]
             for h in range(Q_HEADS)]
    out = jnp.concatenate(parts, axis=0).T
    o_ref[...] = _group_rms_norm(out, g_ref[...]).astype(o_ref.dtype)


def _dense_attn(qt, k, vt, g, batch, seq_len):
    tq = min(DENSE_TQ, seq_len)
    tk = min(DENSE_TK, seq_len)
    return pl.pallas_call(
        functools.partial(_dense_attn_kernel, tk),
        grid=(batch, seq_len // tq),
        in_specs=[pl.BlockSpec((None, GROUP_WIDTH, tq), lambda b, i: (b, 0, i)),
                  pl.BlockSpec((None, seq_len, KV_WIDTH), lambda b, i: (b, 0, 0)),
                  pl.BlockSpec((None, KV_WIDTH, seq_len), lambda b, i: (b, 0, 0)),
                  _resident((1, GROUP_WIDTH), lambda b, i: (0, 0))],
        out_specs=pl.BlockSpec((None, tq, GROUP_WIDTH), lambda b, i: (b, i, 0)),
        out_shape=jax.ShapeDtypeStruct((batch, seq_len, GROUP_WIDTH), BF16),
        scratch_shapes=[pltpu.VMEM((Q_HEADS, KV_WIDTH, tq), BF16),
                        pltpu.VMEM((2, tk, tq), F32),
                        pltpu.VMEM((Q_HEADS, tq), F32),
                        pltpu.VMEM((Q_HEADS, tq), F32),
                        pltpu.VMEM((GROUP_WIDTH, tq), F32)],
        compiler_params=_params("parallel", "parallel"),
        name="dense_attn",
    )(qt, k, vt, g)


FF_CHUNK = 1024


def _out_mlp_kernel(ma_ref, mb_ref, mc_ref, md_ref, h_ref, wo_ref, g1_ref, b1_ref,
                    w1_ref, w2_ref, g2_ref, b2_ref, o_ref):
    mix = jnp.concatenate([r[...].astype(BF16) for r in (ma_ref, mb_ref, mc_ref, md_ref)], axis=-1)
    h1 = _layer_norm(DN_ALPHA * h_ref[...] + _dot(mix, wo_ref[...]), g1_ref[...], b1_ref[...])
    h1_bf = h1.astype(BF16)
    ffn = jnp.zeros(h1.shape, F32)
    for c in range(D_FF // FF_CHUNK):
        cols = slice(c * FF_CHUNK, (c + 1) * FF_CHUNK)
        u = jnp.maximum(_dot(h1_bf, w1_ref[:, cols]), 0.0)
        ffn = ffn + _dot((u * u).astype(BF16), w2_ref[cols, :])
    o_ref[...] = _layer_norm(DN_ALPHA * h1 + ffn, g2_ref[...], b2_ref[...])


def _out_mlp(mixes, h, w_out, g1, b1, w1, w2, g2, b2):
    rows = h.shape[0]
    tm = min(512, rows)
    row_spec = lambda width: pl.BlockSpec((tm, width), lambda i: (i, 0))
    const = lambda shape: _resident(shape, lambda i: (0, 0))
    vec = const((1, D_MODEL))
    return pl.pallas_call(
        _out_mlp_kernel,
        grid=(rows // tm,),
        in_specs=[row_spec(GROUP_WIDTH)] * 4 + [row_spec(D_MODEL), const((D_MODEL, D_MODEL)),
                                                 vec, vec, const((D_MODEL, D_FF)),
                                                 const((D_FF, D_MODEL)), vec, vec],
        out_specs=row_spec(D_MODEL),
        out_shape=jax.ShapeDtypeStruct((rows, D_MODEL), F32),
        compiler_params=_params("parallel"),
        name="out_mlp",
    )(*mixes, h, w_out, g1, b1, w1, w2, g2, b2)


def kernel(x, ln_in_g, ln_in_b, w_in, conv_w, sink, qn_g, kn_g, grp_g, w_out, ln1_g, ln1_b,
           w1, w2, ln2_g, ln2_b):
    batch, seq_len, _ = x.shape
    rows = batch * seq_len
    depth = w_in.shape[0]
    row_vec = lambda v: v.reshape(1, -1)

    h = x.reshape(rows, D_MODEL)
    for l in range(depth):
        first = l == 0
        outs = _in_proj(h, row_vec(ln_in_g), row_vec(ln_in_b), w_in[l].astype(BF16), first)
        if first:
            h, pa, pb, pc, pd = outs
        else:
            pa, pb, pc, pd = outs
        g = [row_vec(grp_g[l, i * GROUP_WIDTH:(i + 1) * GROUP_WIDTH]) for i in range(4)]
        mix_a = _window_attn(pa, sink[l], g[0], batch, seq_len)
        mix_b = _short_conv(pb, conv_w[l], g[1], batch, seq_len)
        mix_c = _fourier(pc, g[2], batch, seq_len)
        qt, kd, vt = _dense_prep(pd, qn_g[l], kn_g[l], batch, seq_len)
        mix_d = _dense_attn(qt, kd, vt, g[3], batch, seq_len)
        mixes = [m.reshape(rows, GROUP_WIDTH) for m in (mix_a, mix_b, mix_c, mix_d)]
        h = _out_mlp(mixes, h, w_out[l].astype(BF16), row_vec(ln1_g[l]), row_vec(ln1_b[l]),
                     w1[l].astype(BF16), w2[l].astype(BF16), row_vec(ln2_g[l]), row_vec(ln2_b[l]))
    return h.reshape(batch, seq_len, D_MODEL)
```

```python
import functools
import math

import numpy as np
import jax
import jax.numpy as jnp
from jax import lax
from jax.experimental import pallas as pl
from jax.experimental.pallas import tpu as pltpu

F32 = jnp.float32
BF16 = jnp.bfloat16

D_MODEL = 1024
HEAD_DIM = 64
GROUP_WIDTH = 256
Q_HEADS = 4
KV_HEADS = 2
GROUP = Q_HEADS // KV_HEADS
KV_WIDTH = KV_HEADS * HEAD_DIM
CONV_WIDTH = 3
WINDOW = 128
BLOCK = 128
GRID_W = 64
AXIS_DIM = HEAD_DIM // 2
ROPE_THETA = 10000.0
D_FF = 4 * D_MODEL
LN_EPS = 1e-5
RMS_EPS = 1e-6
DEPTH = 2
DN_ALPHA = (2 * DEPTH) ** 0.25
NEG_INF = -1e30
LOG2_E = math.log2(math.e)
ALIBI_SLOPES = tuple(2.0 ** (-8.0 * (h + 1) / Q_HEADS) for h in range(Q_HEADS))

V7X_VMEM_LIMIT_BYTES = 56 * 1024 * 1024
LANES = 128
SUBLANES = 8
BF16_SUBLANES = 16
VT_ROWS = HEAD_DIM + BF16_SUBLANES
ROW_TILE = 512


def _params(*semantics):
    return pltpu.CompilerParams(dimension_semantics=semantics,
                                vmem_limit_bytes=V7X_VMEM_LIMIT_BYTES)


def _resident(block_shape, index_map):
    return pl.BlockSpec(block_shape, index_map, pipeline_mode=pl.Buffered(1))


def _dot(a, b):
    return jnp.dot(a, b, preferred_element_type=F32)


def _split_bf16(x):
    hi = x.astype(BF16)
    lo = (x - hi.astype(F32)).astype(BF16)
    return hi, lo


def _dot3(a_hi, a_lo, b_hi, b_lo):
    return _dot(a_hi, b_hi) + _dot(a_hi, b_lo) + _dot(a_lo, b_hi)


def _layer_norm(x, g, b):
    mu = jnp.mean(x, axis=-1, keepdims=True)
    xc = x - mu
    var = jnp.mean(xc * xc, axis=-1, keepdims=True)
    return xc * lax.rsqrt(var + LN_EPS) * g + b


def _group_rms_norm(x, g):
    ms = jnp.mean(x * x, axis=-1, keepdims=True)
    return x * lax.rsqrt(ms + RMS_EPS) * g


A_COLS = 512
B_COLS = 768
C_COLS = 256
D_COLS = 512
D_IN_PROJ = A_COLS + B_COLS + C_COLS + D_COLS


def _rope_tables(seq_len):
    pos = jnp.arange(seq_len)
    inv_freq = ROPE_THETA ** (-jnp.arange(0, AXIS_DIM, 2, dtype=F32) / AXIS_DIM)
    ang_row = (pos // GRID_W).astype(F32)[:, None] * inv_freq[None, :]
    ang_col = (pos % GRID_W).astype(F32)[:, None] * inv_freq[None, :]
    ang = jnp.concatenate([ang_row, ang_row, ang_col, ang_col], axis=-1)
    half = AXIS_DIM // 2
    sign = np.tile(np.concatenate([-np.ones(half), np.ones(half)]), HEAD_DIM // AXIS_DIM)
    cos = jnp.tile(jnp.cos(ang), (1, LANES // HEAD_DIM))
    sin_signed = jnp.tile(jnp.sin(ang) * jnp.asarray(sign, F32), (1, LANES // HEAD_DIM))
    return cos, sin_signed


def _head_norm_rope(x, gain, cos, sin_signed, head_mean):
    width = x.shape[-1]
    sq_hi, sq_lo = _split_bf16(x * x)
    ms = _dot(sq_hi, head_mean) + _dot(sq_lo, head_mean)
    xn = x * lax.rsqrt(ms + RMS_EPS) * gain
    reps = width // LANES
    if reps > 1:
        cos = jnp.concatenate([cos] * reps, axis=-1)
        sin_signed = jnp.concatenate([sin_signed] * reps, axis=-1)
    half = AXIS_DIM // 2
    lane = lax.broadcasted_iota(jnp.int32, x.shape, 1)
    first_half = (lane % AXIS_DIM) < half
    partner = jnp.where(first_half, pltpu.roll(xn, width - half, axis=1),
                        pltpu.roll(xn, half, axis=1))
    return xn * cos + partner * sin_signed


def _store_attn_operands(q, k, v, qt_ref, k_ref, vt_ref):
    rows = q.shape[0]
    qt_ref[...] = (q * (HEAD_DIM ** -0.5 * LOG2_E)).T.astype(BF16)
    k_ref[...] = k.astype(BF16)
    vt = v.T.astype(BF16)
    pad = (lax.broadcasted_iota(jnp.int32, (VT_ROWS - HEAD_DIM, rows), 0) == 0).astype(BF16)
    vt_ref[...] = jnp.concatenate(
        [piece for kv in range(KV_HEADS)
         for piece in (vt[kv * HEAD_DIM:(kv + 1) * HEAD_DIM], pad)], axis=0)


def _in_proj_kernel(apply_ln, x_ref, g_ref, b_ref, w_ref, cos_ref, sin_ref, gq_ref, gk_ref,
                    hm_ref, *out_refs):
    x = x_ref[...]
    if apply_ln:
        h_ref, *out_refs = out_refs
        x = _layer_norm(x, g_ref[...], b_ref[...])
        h_ref[...] = x
    qta_ref, ka_ref, vta_ref, pb_ref, pc_ref, qtd_ref, kd_ref, vtd_ref = out_refs
    proj = _dot(x.astype(BF16), w_ref[...])
    a_q, a_k, a_v = (proj[:, :GROUP_WIDTH], proj[:, GROUP_WIDTH:GROUP_WIDTH + KV_WIDTH],
                     proj[:, GROUP_WIDTH + KV_WIDTH:A_COLS])
    _store_attn_operands(a_q, a_k, a_v, qta_ref, ka_ref, vta_ref)
    pb_ref[...] = proj[:, A_COLS:A_COLS + B_COLS]
    pc_ref[...] = proj[:, A_COLS + B_COLS:A_COLS + B_COLS + C_COLS]
    d0 = A_COLS + B_COLS + C_COLS
    d_q, d_k, d_v = (proj[:, d0:d0 + GROUP_WIDTH],
                     proj[:, d0 + GROUP_WIDTH:d0 + GROUP_WIDTH + KV_WIDTH],
                     proj[:, d0 + GROUP_WIDTH + KV_WIDTH:])
    cos, sin_signed, hm = cos_ref[...], sin_ref[...], hm_ref[...]
    d_q = _head_norm_rope(d_q, gq_ref[...], cos, sin_signed, hm)
    d_k = _head_norm_rope(d_k, gk_ref[...], cos, sin_signed, hm[:KV_WIDTH, :KV_WIDTH])
    _store_attn_operands(d_q, d_k, d_v, qtd_ref, kd_ref, vtd_ref)


def _in_proj(x, g, b, w, qn_g, kn_g, apply_ln, batch, seq_len):
    tm = min(ROW_TILE, seq_len)
    cos, sin_signed = _rope_tables(seq_len)
    gq = jnp.tile(qn_g, GROUP_WIDTH // HEAD_DIM).reshape(1, GROUP_WIDTH)
    gk = jnp.tile(kn_g, KV_WIDTH // HEAD_DIM).reshape(1, KV_WIDTH)
    head_mean = jnp.asarray(
        np.kron(np.eye(GROUP_WIDTH // HEAD_DIM), np.full((HEAD_DIM, HEAD_DIM), 1.0 / HEAD_DIM)),
        BF16)
    rows_spec = lambda width: pl.BlockSpec((None, tm, width), lambda bi, i: (bi, i, 0))
    cols_spec = lambda height: pl.BlockSpec((None, height, tm), lambda bi, i: (bi, 0, i))
    table_spec = pl.BlockSpec((tm, LANES), lambda bi, i: (i, 0))
    const = lambda shape: _resident(shape, lambda bi, i: (0, 0))
    rows_shape = lambda width, dt: jax.ShapeDtypeStruct((batch, seq_len, width), dt)
    cols_shape = lambda height: jax.ShapeDtypeStruct((batch, height, seq_len), BF16)
    attn_specs = [cols_spec(GROUP_WIDTH), rows_spec(KV_WIDTH), cols_spec(KV_HEADS * VT_ROWS)]
    attn_shapes = [cols_shape(GROUP_WIDTH), rows_shape(KV_WIDTH, BF16),
                   cols_shape(KV_HEADS * VT_ROWS)]
    out_specs = attn_specs + [rows_spec(B_COLS), rows_spec(C_COLS)] + attn_specs
    out_shape = attn_shapes + [rows_shape(B_COLS, F32), rows_shape(C_COLS, F32)] + attn_shapes
    if apply_ln:
        out_specs = [rows_spec(D_MODEL)] + out_specs
        out_shape = [rows_shape(D_MODEL, F32)] + out_shape
    return pl.pallas_call(
        functools.partial(_in_proj_kernel, apply_ln),
        grid=(batch, seq_len // tm),
        in_specs=[rows_spec(D_MODEL), const((1, D_MODEL)), const((1, D_MODEL)),
                  const((D_MODEL, D_IN_PROJ)), table_spec, table_spec,
                  const((1, GROUP_WIDTH)), const((1, KV_WIDTH)), const((GROUP_WIDTH, GROUP_WIDTH))],
        out_specs=out_specs,
        out_shape=out_shape,
        compiler_params=_params("parallel", "parallel"),
        name="in_proj_ln" if apply_ln else "in_proj",
    )(x, g, b, w, cos, sin_signed, gq, gk, head_mean)


WINDOW_TQ = 2048


def _window_bias_table():
    key = np.arange(3 * BLOCK)[:, None]
    query = np.arange(BLOCK)[None, :]
    dist = np.abs(BLOCK + query - key)
    per_head = [np.where(dist <= WINDOW, -slope * LOG2_E * dist, NEG_INF) for slope in ALIBI_SLOPES]
    groups = [np.concatenate(per_head[kv * GROUP:(kv + 1) * GROUP], axis=1) for kv in range(KV_HEADS)]
    return jnp.asarray(np.stack(groups), F32)


def _window_attn_kernel(qt_ref, km_ref, kp_ref, kn_ref, vm_ref, vp_ref, vn_ref, bias_ref,
                        sink_ref, g_ref, o_ref, kall_ref, vall_ref, s_ref):
    tq = qt_ref.shape[1]
    n_blk = tq // BLOCK
    i = pl.program_id(1)
    before_start = jnp.where(i == 0, NEG_INF, 0.0)
    after_end = jnp.where(i == pl.num_programs(1) - 1, NEG_INF, 0.0)

    kall_ref[0:BLOCK, :] = kp_ref[...]
    kall_ref[BLOCK:BLOCK + tq, :] = km_ref[...]
    kall_ref[BLOCK + tq:, :] = kn_ref[...]
    vall_ref[:, 0:BLOCK] = vp_ref[...]
    vall_ref[:, BLOCK:BLOCK + tq] = vm_ref[...]
    vall_ref[:, BLOCK + tq:] = vn_ref[...]
    zeros = jnp.zeros((HEAD_DIM, BLOCK), BF16)
    g = g_ref[...]

    def scores(jb, kv):
        cols = slice(jb * BLOCK, (jb + 1) * BLOCK)
        halves = []
        for h in range(kv * GROUP, (kv + 1) * GROUP):
            qh = qt_ref[h * HEAD_DIM:(h + 1) * HEAD_DIM, cols]
            halves.append(jnp.concatenate([qh, zeros] if kv == 0 else [zeros, qh], axis=0))
        kw = kall_ref[jb * BLOCK:(jb + 3) * BLOCK, :]
        s = _dot(kw, jnp.concatenate(halves, axis=1)) + bias_ref[kv]
        if jb == 0:
            s = jnp.concatenate([s[:BLOCK] + before_start, s[BLOCK:]], axis=0)
        if jb == n_blk - 1:
            s = jnp.concatenate([s[:2 * BLOCK], s[2 * BLOCK:] + after_end], axis=0)
        s_ref[kv] = s
        return jnp.max(s, axis=0, keepdims=True)

    def softmax_pv(jb, kv, s_max):
        sink = sink_ref[kv]
        m = jnp.maximum(s_max, sink)
        p = jnp.exp2(s_ref[kv] - m).astype(BF16)
        vt = vall_ref[kv * VT_ROWS:(kv + 1) * VT_ROWS, jb * BLOCK:(jb + 3) * BLOCK]
        return _dot(vt, p), jnp.exp2(sink - m)

    outs = []

    def finish(jb, kv, pv, sink_term):
        o = pv[:HEAD_DIM] / (pv[HEAD_DIM:HEAD_DIM + 1] + sink_term)
        outs.extend([o[:, :BLOCK], o[:, BLOCK:]])
        if kv == KV_HEADS - 1:
            out = jnp.concatenate(outs, axis=0).T
            o_ref[jb * BLOCK:(jb + 1) * BLOCK, :] = _group_rms_norm(out, g).astype(o_ref.dtype)
            outs.clear()

    units = [(jb, kv) for jb in range(n_blk) for kv in range(KV_HEADS)]
    s_max = scores(*units[0])
    pending = None
    for idx, unit in enumerate(units):
        nxt = scores(*units[idx + 1]) if idx + 1 < len(units) else None
        pv_parts = softmax_pv(*unit, s_max)
        if pending is not None:
            finish(*pending)
        pending = (*unit, *pv_parts)
        s_max = nxt
    finish(*pending)


def _window_attn(qt, k, vt, sink, g, batch, seq_len):
    tq = min(WINDOW_TQ, seq_len)
    per_tile = tq // BLOCK
    n_blocks = seq_len // BLOCK
    sink_rows = jnp.repeat(sink * LOG2_E, BLOCK).reshape(KV_HEADS, 1, GROUP * BLOCK)
    prev_idx = lambda i: jnp.maximum(i * per_tile - 1, 0)
    next_idx = lambda i: jnp.minimum((i + 1) * per_tile, n_blocks - 1)
    v_rows = KV_HEADS * VT_ROWS
    const3 = lambda shape: _resident(shape, lambda b, i: (0, 0, 0))
    return pl.pallas_call(
        _window_attn_kernel,
        grid=(batch, seq_len // tq),
        in_specs=[pl.BlockSpec((None, GROUP_WIDTH, tq), lambda b, i: (b, 0, i)),
                  pl.BlockSpec((None, tq, KV_WIDTH), lambda b, i: (b, i, 0)),
                  pl.BlockSpec((None, BLOCK, KV_WIDTH), lambda b, i: (b, prev_idx(i), 0)),
                  pl.BlockSpec((None, BLOCK, KV_WIDTH), lambda b, i: (b, next_idx(i), 0)),
                  pl.BlockSpec((None, v_rows, tq), lambda b, i: (b, 0, i)),
                  pl.BlockSpec((None, v_rows, BLOCK), lambda b, i: (b, 0, prev_idx(i))),
                  pl.BlockSpec((None, v_rows, BLOCK), lambda b, i: (b, 0, next_idx(i))),
                  const3((KV_HEADS, 3 * BLOCK, GROUP * BLOCK)),
                  const3((KV_HEADS, 1, GROUP * BLOCK)),
                  _resident((1, GROUP_WIDTH), lambda b, i: (0, 0))],
        out_specs=pl.BlockSpec((None, tq, GROUP_WIDTH), lambda b, i: (b, i, 0)),
        out_shape=jax.ShapeDtypeStruct((batch, seq_len, GROUP_WIDTH), BF16),
        scratch_shapes=[pltpu.VMEM((tq + 2 * BLOCK, KV_WIDTH), BF16),
                        pltpu.VMEM((v_rows, tq + 2 * BLOCK), BF16),
                        pltpu.VMEM((KV_HEADS, 3 * BLOCK, GROUP * BLOCK), F32)],
        compiler_params=_params("parallel", "parallel"),
        name="window_attn",
    )(qt, k, k, k, vt, vt, vt, _window_bias_table(), sink_rows, g)


def _short_conv_kernel(u_ref, gb_ref, gc_ref, up_ref, un_ref, cp_ref, cn_ref, w_ref, g_ref,
                       o_ref):
    tm = u_ref.shape[0]
    i = pl.program_id(1)
    last = pl.num_programs(1) - 1
    z = gc_ref[...] * u_ref[...]
    z_before = jnp.where(i > 0, cp_ref[SUBLANES - 1:SUBLANES, :] * up_ref[SUBLANES - 1:SUBLANES, :],
                         0.0)
    z_after = jnp.where(i < last, cn_ref[0:1, :] * un_ref[0:1, :], 0.0)
    row = lax.broadcasted_iota(jnp.int32, z.shape, 0)
    z_m1 = jnp.where(row == 0, z_before, pltpu.roll(z, 1, axis=0))
    z_p1 = jnp.where(row == tm - 1, z_after, pltpu.roll(z, tm - 1, axis=0))
    w = w_ref[...]
    y = gb_ref[...] * (w[0:1, :] * z_m1 + w[1:2, :] * z + w[2:3, :] * z_p1)
    o_ref[...] = _group_rms_norm(y, g_ref[...]).astype(o_ref.dtype)


def _short_conv(pb, conv_w, g, batch, seq_len):
    tm = min(ROW_TILE, seq_len)
    per_tile = tm // SUBLANES
    n_row_blocks = seq_len // SUBLANES
    u_col, gb_col, gc_col = 0, 1, 2

    def main(col):
        return pl.BlockSpec((None, tm, GROUP_WIDTH), lambda b, i: (b, i, col))

    def prev(col):
        return pl.BlockSpec((None, SUBLANES, GROUP_WIDTH),
                            lambda b, i: (b, jnp.maximum(i * per_tile - 1, 0), col))

    def nxt(col):
        return pl.BlockSpec((None, SUBLANES, GROUP_WIDTH),
                            lambda b, i: (b, jnp.minimum((i + 1) * per_tile, n_row_blocks - 1), col))

    return pl.pallas_call(
        _short_conv_kernel,
        grid=(batch, seq_len // tm),
        in_specs=[main(u_col), main(gb_col), main(gc_col),
                  prev(u_col), nxt(u_col), prev(gc_col), nxt(gc_col),
                  _resident((CONV_WIDTH, GROUP_WIDTH), lambda b, i: (0, 0)),
                  _resident((1, GROUP_WIDTH), lambda b, i: (0, 0))],
        out_specs=pl.BlockSpec((None, tm, GROUP_WIDTH), lambda b, i: (b, i, 0)),
        out_shape=jax.ShapeDtypeStruct((batch, seq_len, GROUP_WIDTH), BF16),
        compiler_params=_params("parallel", "parallel"),
        name="short_conv",
    )(pb, pb, pb, pb, pb, pb, pb, conv_w, g)


FOURIER_T2_PER_STEP = 8
FOURIER_K1_PER_STEP = SUBLANES


def _dft_cos_sin(n):
    idx = np.arange(n)
    ang = 2.0 * np.pi * ((idx[:, None] * idx[None, :]) % n) / n
    return np.cos(ang), np.sin(ang)


def _fourier_stage1_kernel(x_ref, wh_ref, wl_ref, tc_ref, ts_ref, o_ref):
    n1 = x_ref.shape[0]
    x_hi, x_lo = _split_bf16(x_ref[...])
    a = _dot3(wh_ref[...], wl_ref[...], x_hi, x_lo)
    a_re, a_im = a[:n1], a[n1:]
    tc, ts = tc_ref[...], ts_ref[...]
    o_ref[0] = a_re * tc + a_im * ts
    o_ref[1] = a_im * tc - a_re * ts


def _fourier_stage2_kernel(a_ref, w2h_ref, w2l_ref, w3h_ref, w3l_ref, g_ref, o_ref):
    n2 = a_ref.shape[2]
    for j in range(a_ref.shape[1]):
        a = jnp.concatenate([a_ref[0, j], a_ref[1, j]], axis=0)
        a_hi, a_lo = _split_bf16(a)
        y = _dot3(w2h_ref[...], w2l_ref[...], a_hi, a_lo)
        y2 = jnp.concatenate([y[:n2], y[n2:]], axis=1)
        y_hi, y_lo = _split_bf16(y2)
        z = _dot3(y_hi, y_lo, w3h_ref[...], w3l_ref[...])
        o_ref[:, j, :] = _group_rms_norm(z, g_ref[...]).astype(o_ref.dtype)


def _fourier(pc, g, batch, seq_len):
    n = math.isqrt(seq_len)
    assert n * n == seq_len and n % FOURIER_T2_PER_STEP == 0
    cols = FOURIER_T2_PER_STEP * GROUP_WIDTH

    cos_n, sin_n = _dft_cos_sin(n)
    w1_hi, w1_lo = _split_bf16(jnp.asarray(np.concatenate([cos_n, -sin_n], axis=0), F32))
    w2 = np.block([[cos_n, sin_n], [-sin_n, cos_n]])
    w2_hi, w2_lo = _split_bf16(jnp.asarray(w2, F32))
    cos_c, sin_c = _dft_cos_sin(HEAD_DIM)
    eye = np.eye(GROUP_WIDTH // HEAD_DIM)
    w3 = np.concatenate([np.kron(eye, cos_c), np.kron(eye, sin_c)], axis=0)
    w3_hi, w3_lo = _split_bf16(jnp.asarray(w3 / math.sqrt(seq_len * HEAD_DIM), F32))

    k1 = jnp.arange(n, dtype=jnp.int32)[:, None]
    t2 = jnp.arange(n, dtype=jnp.int32)[None, :]
    theta = (2.0 * math.pi / seq_len) * (k1 * t2).astype(F32)
    tw_cos = jnp.repeat(jnp.cos(theta), GROUP_WIDTH, axis=1)
    tw_sin = jnp.repeat(jnp.sin(theta), GROUP_WIDTH, axis=1)

    const = lambda shape: _resident(shape, lambda b, j: (0, 0))
    stage1 = pl.pallas_call(
        _fourier_stage1_kernel,
        grid=(batch, n // FOURIER_T2_PER_STEP),
        in_specs=[pl.BlockSpec((None, n, cols), lambda b, j: (b, 0, j)),
                  const((2 * n, n)), const((2 * n, n)),
                  pl.BlockSpec((n, cols), lambda b, j: (0, j)),
                  pl.BlockSpec((n, cols), lambda b, j: (0, j))],
        out_specs=pl.BlockSpec((None, 2, n, cols), lambda b, j: (b, 0, 0, j)),
        out_shape=jax.ShapeDtypeStruct((batch, 2, n, n * GROUP_WIDTH), F32),
        compiler_params=_params("parallel", "parallel"),
        name="fourier_stage1",
    )(pc.reshape(batch, n, n * GROUP_WIDTH), w1_hi, w1_lo, tw_cos, tw_sin)

    out = pl.pallas_call(
        _fourier_stage2_kernel,
        grid=(batch, n // FOURIER_K1_PER_STEP),
        in_specs=[pl.BlockSpec((None, 2, FOURIER_K1_PER_STEP, n, GROUP_WIDTH),
                               lambda b, j: (b, 0, j, 0, 0)),
                  const((2 * n, 2 * n)), const((2 * n, 2 * n)),
                  const((2 * GROUP_WIDTH, GROUP_WIDTH)), const((2 * GROUP_WIDTH, GROUP_WIDTH)),
                  const((1, GROUP_WIDTH))],
        out_specs=pl.BlockSpec((None, n, FOURIER_K1_PER_STEP, GROUP_WIDTH),
                               lambda b, j: (b, 0, j, 0)),
        out_shape=jax.ShapeDtypeStruct((batch, n, n, GROUP_WIDTH), F32),
        compiler_params=_params("parallel", "parallel"),
        name="fourier_stage2",
    )(stage1.reshape(batch, 2, n, n, GROUP_WIDTH), w2_hi, w2_lo, w3_hi, w3_lo, g)
    return out.reshape(batch, seq_len, GROUP_WIDTH)


DENSE_TQ = 512
DENSE_TK = 512


def _dense_attn_kernel(tk, qt_ref, k_ref, vt_ref, g_ref, o_ref, qpad_ref, s_ref, m_ref, acc_ref):
    tq = qt_ref.shape[1]
    n_chunks = k_ref.shape[0] // tk

    m_ref[...] = jnp.full(m_ref.shape, -jnp.inf, F32)
    acc_ref[...] = jnp.zeros(acc_ref.shape, F32)
    zeros = jnp.zeros((HEAD_DIM, tq), BF16)
    for h in range(Q_HEADS):
        qh = qt_ref[h * HEAD_DIM:(h + 1) * HEAD_DIM, :]
        qpad_ref[h] = jnp.concatenate([qh, zeros] if h < GROUP else [zeros, qh], axis=0)

    def scores(j, h):
        start = pl.multiple_of(j * tk, tk)
        s = _dot(k_ref[pl.ds(start, tk), :], qpad_ref[h])
        s_ref[h % 2] = s
        return jnp.max(s, axis=0, keepdims=True)

    def softmax_pv(j, h, s_max):
        start = pl.multiple_of(j * tk, tk)
        m_old = m_ref[h:h + 1, :]
        m_new = jnp.maximum(m_old, s_max)
        p = jnp.exp2(s_ref[h % 2] - m_new).astype(BF16)
        alpha = jnp.exp2(m_old - m_new)
        kv = h // GROUP
        vt = vt_ref[kv * VT_ROWS:(kv + 1) * VT_ROWS, pl.ds(start, tk)]
        rows = slice(h * VT_ROWS, (h + 1) * VT_ROWS)
        acc_ref[rows, :] = alpha * acc_ref[rows, :] + _dot(vt, p)
        m_ref[h:h + 1, :] = m_new

    def chunk(j, s_max):
        for h in range(Q_HEADS):
            if h + 1 < Q_HEADS:
                nxt = scores(j, h + 1)
            else:
                nxt = scores(jnp.minimum(j + 1, n_chunks - 1), 0)
            softmax_pv(j, h, s_max)
            s_max = nxt
        return s_max

    lax.fori_loop(0, n_chunks, chunk, scores(0, 0), unroll=2)

    parts = []
    for h in range(Q_HEADS):
        row_sum = acc_ref[h * VT_ROWS + HEAD_DIM:h * VT_ROWS + HEAD_DIM + 1, :]
        parts.append(acc_ref[h * VT_ROWS:h * VT_ROWS + HEAD_DIM, :] / row_sum)
    out = jnp.concatenate(parts, axis=0).T
    o_ref[...] = _group_rms_norm(out, g_ref[...]).astype(o_ref.dtype)


def _dense_attn(qt, k, vt, g, batch, seq_len):
    tq = min(DENSE_TQ, seq_len)
    tk = min(DENSE_TK, seq_len)
    return pl.pallas_call(
        functools.partial(_dense_attn_kernel, tk),
        grid=(batch, seq_len // tq),
        in_specs=[pl.BlockSpec((None, GROUP_WIDTH, tq), lambda b, i: (b, 0, i)),
                  pl.BlockSpec((None, seq_len, KV_WIDTH), lambda b, i: (b, 0, 0)),
                  pl.BlockSpec((None, KV_HEADS * VT_ROWS, seq_len), lambda b, i: (b, 0, 0)),
                  _resident((1, GROUP_WIDTH), lambda b, i: (0, 0))],
        out_specs=pl.BlockSpec((None, tq, GROUP_WIDTH), lambda b, i: (b, i, 0)),
        out_shape=jax.ShapeDtypeStruct((batch, seq_len, GROUP_WIDTH), BF16),
        scratch_shapes=[pltpu.VMEM((Q_HEADS, KV_WIDTH, tq), BF16),
                        pltpu.VMEM((2, tk, tq), F32),
                        pltpu.VMEM((Q_HEADS, tq), F32),
                        pltpu.VMEM((Q_HEADS * VT_ROWS, tq), F32)],
        compiler_params=_params("parallel", "parallel"),
        name="dense_attn",
    )(qt, k, vt, g)


FF_CHUNK = 1024


def _out_mlp_kernel(ma_ref, mb_ref, mc_ref, md_ref, h_ref, wo_ref, g1_ref, b1_ref,
                    w1_ref, w2_ref, g2_ref, b2_ref, o_ref):
    mix = jnp.concatenate([r[...].astype(BF16) for r in (ma_ref, mb_ref, mc_ref, md_ref)], axis=-1)
    h1 = _layer_norm(DN_ALPHA * h_ref[...] + _dot(mix, wo_ref[...]), g1_ref[...], b1_ref[...])
    h1_bf = h1.astype(BF16)
    ffn = jnp.zeros(h1.shape, F32)
    for c in range(D_FF // FF_CHUNK):
        cols = slice(c * FF_CHUNK, (c + 1) * FF_CHUNK)
        u = jnp.maximum(_dot(h1_bf, w1_ref[:, cols]), 0.0)
        ffn = ffn + _dot((u * u).astype(BF16), w2_ref[cols, :])
    o_ref[...] = _layer_norm(DN_ALPHA * h1 + ffn, g2_ref[...], b2_ref[...])


def _out_mlp(mixes, h, w_out, g1, b1, w1, w2, g2, b2):
    rows = h.shape[0]
    tm = min(ROW_TILE, rows)
    row_spec = lambda width: pl.BlockSpec((tm, width), lambda i: (i, 0))
    const = lambda shape: _resident(shape, lambda i: (0, 0))
    vec = const((1, D_MODEL))
    return pl.pallas_call(
        _out_mlp_kernel,
        grid=(rows // tm,),
        in_specs=[row_spec(GROUP_WIDTH)] * 4 + [row_spec(D_MODEL), const((D_MODEL, D_MODEL)),
                                                 vec, vec, const((D_MODEL, D_FF)),
                                                 const((D_FF, D_MODEL)), vec, vec],
        out_specs=row_spec(D_MODEL),
        out_shape=jax.ShapeDtypeStruct((rows, D_MODEL), F32),
        compiler_params=_params("parallel"),
        name="out_mlp",
    )(*mixes, h, w_out, g1, b1, w1, w2, g2, b2)


def kernel(x, ln_in_g, ln_in_b, w_in, conv_w, sink, qn_g, kn_g, grp_g, w_out, ln1_g, ln1_b,
           w1, w2, ln2_g, ln2_b):
    batch, seq_len, _ = x.shape
    rows = batch * seq_len
    depth = w_in.shape[0]
    row_vec = lambda v: v.reshape(1, -1)

    h = x
    for l in range(depth):
        first = l == 0
        outs = _in_proj(h, row_vec(ln_in_g), row_vec(ln_in_b), w_in[l].astype(BF16), qn_g[l],
                        kn_g[l], first, batch, seq_len)
        if first:
            h, *outs = outs
        qta, ka, vta, pb, pc, qtd, kd, vtd = outs
        g = [row_vec(grp_g[l, i * GROUP_WIDTH:(i + 1) * GROUP_WIDTH]) for i in range(4)]
        mix_a = _window_attn(qta, ka, vta, sink[l], g[0], batch, seq_len)
        mix_b = _short_conv(pb, conv_w[l], g[1], batch, seq_len)
        mix_c = _fourier(pc, g[2], batch, seq_len)
        mix_d = _dense_attn(qtd, kd, vtd, g[3], batch, seq_len)
        mixes = [m.reshape(rows, GROUP_WIDTH) for m in (mix_a, mix_b, mix_c, mix_d)]
        h = _out_mlp(mixes, h.reshape(rows, D_MODEL), w_out[l].astype(BF16), row_vec(ln1_g[l]),
                     row_vec(ln1_b[l]), w1[l].astype(BF16), w2[l].astype(BF16), row_vec(ln2_g[l]),
                     row_vec(ln2_b[l])).reshape(batch, seq_len, D_MODEL)
    return h
```

```python
import functools
import math

import numpy as np
import jax
import jax.numpy as jnp
from jax import lax
from jax.experimental import pallas as pl
from jax.experimental.pallas import tpu as pltpu

F32 = jnp.float32
BF16 = jnp.bfloat16

D_MODEL = 1024
HEAD_DIM = 64
GROUP_WIDTH = 256
Q_HEADS = 4
KV_HEADS = 2
GROUP = Q_HEADS // KV_HEADS
KV_WIDTH = KV_HEADS * HEAD_DIM
CONV_WIDTH = 3
WINDOW = 128
BLOCK = 128
GRID_W = 64
AXIS_DIM = HEAD_DIM // 2
ROPE_THETA = 10000.0
D_FF = 4 * D_MODEL
LN_EPS = 1e-5
RMS_EPS = 1e-6
DEPTH = 2
DN_ALPHA = (2 * DEPTH) ** 0.25
NEG_INF = -1e30
LOG2_E = math.log2(math.e)
ALIBI_SLOPES = tuple(2.0 ** (-8.0 * (h + 1) / Q_HEADS) for h in range(Q_HEADS))

V7X_VMEM_LIMIT_BYTES = 56 * 1024 * 1024
LANES = 128
SUBLANES = 8
BF16_SUBLANES = 16
VT_ROWS = HEAD_DIM + BF16_SUBLANES
ROW_TILE = 512


def _params(*semantics):
    return pltpu.CompilerParams(dimension_semantics=semantics,
                                vmem_limit_bytes=V7X_VMEM_LIMIT_BYTES)


def _resident(block_shape, index_map):
    return pl.BlockSpec(block_shape, index_map, pipeline_mode=pl.Buffered(1))


def _dot(a, b):
    return jnp.dot(a, b, preferred_element_type=F32)


def _split_bf16(x):
    hi = x.astype(BF16)
    lo = (x - hi.astype(F32)).astype(BF16)
    return hi, lo


def _dot3(a_hi, a_lo, b_hi, b_lo):
    return _dot(a_hi, b_hi) + _dot(a_hi, b_lo) + _dot(a_lo, b_hi)


def _layer_norm(x, g, b):
    mu = jnp.mean(x, axis=-1, keepdims=True)
    xc = x - mu
    var = jnp.mean(xc * xc, axis=-1, keepdims=True)
    return xc * lax.rsqrt(var + LN_EPS) * g + b


def _group_rms_norm(x, g):
    ms = jnp.mean(x * x, axis=-1, keepdims=True)
    return x * lax.rsqrt(ms + RMS_EPS) * g


A_COLS = 512
B_COLS = 768
C_COLS = 256
D_COLS = 512
D_IN_PROJ = A_COLS + B_COLS + C_COLS + D_COLS


def _rope_tables(seq_len):
    pos = jnp.arange(seq_len)
    inv_freq = ROPE_THETA ** (-jnp.arange(0, AXIS_DIM, 2, dtype=F32) / AXIS_DIM)
    ang_row = (pos // GRID_W).astype(F32)[:, None] * inv_freq[None, :]
    ang_col = (pos % GRID_W).astype(F32)[:, None] * inv_freq[None, :]
    ang = jnp.concatenate([ang_row, ang_row, ang_col, ang_col], axis=-1)
    half = AXIS_DIM // 2
    sign = np.tile(np.concatenate([-np.ones(half), np.ones(half)]), HEAD_DIM // AXIS_DIM)
    cos = jnp.tile(jnp.cos(ang), (1, LANES // HEAD_DIM))
    sin_signed = jnp.tile(jnp.sin(ang) * jnp.asarray(sign, F32), (1, LANES // HEAD_DIM))
    return cos, sin_signed


def _head_norm_rope(x, gain, cos, sin_signed, head_mean):
    width = x.shape[-1]
    sq_hi, sq_lo = _split_bf16(x * x)
    ms = _dot(sq_hi, head_mean) + _dot(sq_lo, head_mean)
    xn = x * lax.rsqrt(ms + RMS_EPS) * gain
    reps = width // LANES
    if reps > 1:
        cos = jnp.concatenate([cos] * reps, axis=-1)
        sin_signed = jnp.concatenate([sin_signed] * reps, axis=-1)
    half = AXIS_DIM // 2
    lane = lax.broadcasted_iota(jnp.int32, x.shape, 1)
    first_half = (lane % AXIS_DIM) < half
    partner = jnp.where(first_half, pltpu.roll(xn, width - half, axis=1),
                        pltpu.roll(xn, half, axis=1))
    return xn * cos + partner * sin_signed


def _store_attn_operands(q, k, v, qt_ref, k_ref, vt_ref):
    rows = q.shape[0]
    qt_ref[...] = (q * (HEAD_DIM ** -0.5 * LOG2_E)).T.astype(BF16)
    k_ref[...] = k.astype(BF16)
    vt = v.T.astype(BF16)
    pad = (lax.broadcasted_iota(jnp.int32, (VT_ROWS - HEAD_DIM, rows), 0) == 0).astype(BF16)
    vt_ref[...] = jnp.concatenate(
        [piece for kv in range(KV_HEADS)
         for piece in (vt[kv * HEAD_DIM:(kv + 1) * HEAD_DIM], pad)], axis=0)


def _in_proj_kernel(apply_ln, x_ref, g_ref, b_ref, w_ref, cos_ref, sin_ref, gq_ref, gk_ref,
                    hm_ref, *out_refs):
    x = x_ref[...]
    if apply_ln:
        h_ref, *out_refs = out_refs
        x = _layer_norm(x, g_ref[...], b_ref[...])
        h_ref[...] = x
    qta_ref, ka_ref, vta_ref, pb_ref, pc_ref, qtd_ref, kd_ref, vtd_ref = out_refs
    proj = _dot(x.astype(BF16), w_ref[...])
    a_q, a_k, a_v = (proj[:, :GROUP_WIDTH], proj[:, GROUP_WIDTH:GROUP_WIDTH + KV_WIDTH],
                     proj[:, GROUP_WIDTH + KV_WIDTH:A_COLS])
    _store_attn_operands(a_q, a_k, a_v, qta_ref, ka_ref, vta_ref)
    pb_ref[...] = proj[:, A_COLS:A_COLS + B_COLS]
    pc_ref[...] = proj[:, A_COLS + B_COLS:A_COLS + B_COLS + C_COLS]
    d0 = A_COLS + B_COLS + C_COLS
    d_q, d_k, d_v = (proj[:, d0:d0 + GROUP_WIDTH],
                     proj[:, d0 + GROUP_WIDTH:d0 + GROUP_WIDTH + KV_WIDTH],
                     proj[:, d0 + GROUP_WIDTH + KV_WIDTH:])
    cos, sin_signed, hm = cos_ref[...], sin_ref[...], hm_ref[...]
    d_q = _head_norm_rope(d_q, gq_ref[...], cos, sin_signed, hm)
    d_k = _head_norm_rope(d_k, gk_ref[...], cos, sin_signed, hm[:KV_WIDTH, :KV_WIDTH])
    _store_attn_operands(d_q, d_k, d_v, qtd_ref, kd_ref, vtd_ref)


def _in_proj(x, g, b, w, qn_g, kn_g, apply_ln, batch, seq_len):
    tm = min(ROW_TILE, seq_len)
    cos, sin_signed = _rope_tables(seq_len)
    gq = jnp.tile(qn_g, GROUP_WIDTH // HEAD_DIM).reshape(1, GROUP_WIDTH)
    gk = jnp.tile(kn_g, KV_WIDTH // HEAD_DIM).reshape(1, KV_WIDTH)
    head_mean = jnp.asarray(
        np.kron(np.eye(GROUP_WIDTH // HEAD_DIM), np.full((HEAD_DIM, HEAD_DIM), 1.0 / HEAD_DIM)),
        BF16)
    rows_spec = lambda width: pl.BlockSpec((None, tm, width), lambda bi, i: (bi, i, 0))
    cols_spec = lambda height: pl.BlockSpec((None, height, tm), lambda bi, i: (bi, 0, i))
    table_spec = pl.BlockSpec((tm, LANES), lambda bi, i: (i, 0))
    const = lambda shape: _resident(shape, lambda bi, i: (0, 0))
    rows_shape = lambda width, dt: jax.ShapeDtypeStruct((batch, seq_len, width), dt)
    cols_shape = lambda height: jax.ShapeDtypeStruct((batch, height, seq_len), BF16)
    attn_specs = [cols_spec(GROUP_WIDTH), rows_spec(KV_WIDTH), cols_spec(KV_HEADS * VT_ROWS)]
    attn_shapes = [cols_shape(GROUP_WIDTH), rows_shape(KV_WIDTH, BF16),
                   cols_shape(KV_HEADS * VT_ROWS)]
    out_specs = attn_specs + [rows_spec(B_COLS), rows_spec(C_COLS)] + attn_specs
    out_shape = attn_shapes + [rows_shape(B_COLS, F32), rows_shape(C_COLS, F32)] + attn_shapes
    if apply_ln:
        out_specs = [rows_spec(D_MODEL)] + out_specs
        out_shape = [rows_shape(D_MODEL, F32)] + out_shape
    return pl.pallas_call(
        functools.partial(_in_proj_kernel, apply_ln),
        grid=(batch, seq_len // tm),
        in_specs=[rows_spec(D_MODEL), const((1, D_MODEL)), const((1, D_MODEL)),
                  const((D_MODEL, D_IN_PROJ)), table_spec, table_spec,
                  const((1, GROUP_WIDTH)), const((1, KV_WIDTH)), const((GROUP_WIDTH, GROUP_WIDTH))],
        out_specs=out_specs,
        out_shape=out_shape,
        compiler_params=_params("parallel", "parallel"),
        name="in_proj_ln" if apply_ln else "in_proj",
    )(x, g, b, w, cos, sin_signed, gq, gk, head_mean)


WINDOW_TQ = 2048


def _window_bias_table():
    key = np.arange(3 * BLOCK)[:, None]
    query = np.arange(BLOCK)[None, :]
    dist = np.abs(BLOCK + query - key)
    per_head = [np.where(dist <= WINDOW, -slope * LOG2_E * dist, NEG_INF) for slope in ALIBI_SLOPES]
    groups = [np.concatenate(per_head[kv * GROUP:(kv + 1) * GROUP], axis=1) for kv in range(KV_HEADS)]
    return jnp.asarray(np.stack(groups), F32)


def _window_attn_kernel(qt_ref, km_ref, kp_ref, kn_ref, vm_ref, vp_ref, vn_ref, bias_ref,
                        sink_ref, g_ref, o_ref, kall_ref, vall_ref, s_ref):
    tq = qt_ref.shape[1]
    n_blk = tq // BLOCK
    i = pl.program_id(1)
    before_start = jnp.where(i == 0, NEG_INF, 0.0)
    after_end = jnp.where(i == pl.num_programs(1) - 1, NEG_INF, 0.0)

    kall_ref[0:BLOCK, :] = kp_ref[...]
    kall_ref[BLOCK:BLOCK + tq, :] = km_ref[...]
    kall_ref[BLOCK + tq:, :] = kn_ref[...]
    vall_ref[:, 0:BLOCK] = vp_ref[...]
    vall_ref[:, BLOCK:BLOCK + tq] = vm_ref[...]
    vall_ref[:, BLOCK + tq:] = vn_ref[...]
    zeros = jnp.zeros((HEAD_DIM, BLOCK), BF16)
    g = g_ref[...]

    def scores(jb, kv):
        cols = slice(jb * BLOCK, (jb + 1) * BLOCK)
        halves = []
        for h in range(kv * GROUP, (kv + 1) * GROUP):
            qh = qt_ref[h * HEAD_DIM:(h + 1) * HEAD_DIM, cols]
            halves.append(jnp.concatenate([qh, zeros] if kv == 0 else [zeros, qh], axis=0))
        kw = kall_ref[jb * BLOCK:(jb + 3) * BLOCK, :]
        s = _dot(kw, jnp.concatenate(halves, axis=1)) + bias_ref[kv]
        if jb == 0:
            s = jnp.concatenate([s[:BLOCK] + before_start, s[BLOCK:]], axis=0)
        if jb == n_blk - 1:
            s = jnp.concatenate([s[:2 * BLOCK], s[2 * BLOCK:] + after_end], axis=0)
        s_ref[kv] = s
        return jnp.max(s, axis=0, keepdims=True)

    def softmax_pv(jb, kv, s_max):
        sink = sink_ref[kv]
        m = jnp.maximum(s_max, sink)
        p = jnp.exp2(s_ref[kv] - m).astype(BF16)
        vt = vall_ref[kv * VT_ROWS:(kv + 1) * VT_ROWS, jb * BLOCK:(jb + 3) * BLOCK]
        return _dot(vt, p), jnp.exp2(sink - m)

    outs = []

    def finish(jb, kv, pv, sink_term):
        o = pv[:HEAD_DIM] / (pv[HEAD_DIM:HEAD_DIM + 1] + sink_term)
        outs.extend([o[:, :BLOCK], o[:, BLOCK:]])
        if kv == KV_HEADS - 1:
            out = jnp.concatenate(outs, axis=0).T
            o_ref[jb * BLOCK:(jb + 1) * BLOCK, :] = _group_rms_norm(out, g).astype(o_ref.dtype)
            outs.clear()

    units = [(jb, kv) for jb in range(n_blk) for kv in range(KV_HEADS)]
    s_max = scores(*units[0])
    pending = None
    for idx, unit in enumerate(units):
        nxt = scores(*units[idx + 1]) if idx + 1 < len(units) else None
        pv_parts = softmax_pv(*unit, s_max)
        if pending is not None:
            finish(*pending)
        pending = (*unit, *pv_parts)
        s_max = nxt
    finish(*pending)


def _window_attn(qt, k, vt, sink, g, batch, seq_len):
    tq = min(WINDOW_TQ, seq_len)
    per_tile = tq // BLOCK
    n_blocks = seq_len // BLOCK
    sink_rows = jnp.repeat(sink * LOG2_E, BLOCK).reshape(KV_HEADS, 1, GROUP * BLOCK)
    prev_idx = lambda i: jnp.maximum(i * per_tile - 1, 0)
    next_idx = lambda i: jnp.minimum((i + 1) * per_tile, n_blocks - 1)
    v_rows = KV_HEADS * VT_ROWS
    const3 = lambda shape: _resident(shape, lambda b, i: (0, 0, 0))
    return pl.pallas_call(
        _window_attn_kernel,
        grid=(batch, seq_len // tq),
        in_specs=[pl.BlockSpec((None, GROUP_WIDTH, tq), lambda b, i: (b, 0, i)),
                  pl.BlockSpec((None, tq, KV_WIDTH), lambda b, i: (b, i, 0)),
                  pl.BlockSpec((None, BLOCK, KV_WIDTH), lambda b, i: (b, prev_idx(i), 0)),
                  pl.BlockSpec((None, BLOCK, KV_WIDTH), lambda b, i: (b, next_idx(i), 0)),
                  pl.BlockSpec((None, v_rows, tq), lambda b, i: (b, 0, i)),
                  pl.BlockSpec((None, v_rows, BLOCK), lambda b, i: (b, 0, prev_idx(i))),
                  pl.BlockSpec((None, v_rows, BLOCK), lambda b, i: (b, 0, next_idx(i))),
                  const3((KV_HEADS, 3 * BLOCK, GROUP * BLOCK)),
                  const3((KV_HEADS, 1, GROUP * BLOCK)),
                  _resident((1, GROUP_WIDTH), lambda b, i: (0, 0))],
        out_specs=pl.BlockSpec((None, tq, GROUP_WIDTH), lambda b, i: (b, i, 0)),
        out_shape=jax.ShapeDtypeStruct((batch, seq_len, GROUP_WIDTH), BF16),
        scratch_shapes=[pltpu.VMEM((tq + 2 * BLOCK, KV_WIDTH), BF16),
                        pltpu.VMEM((v_rows, tq + 2 * BLOCK), BF16),
                        pltpu.VMEM((KV_HEADS, 3 * BLOCK, GROUP * BLOCK), F32)],
        compiler_params=_params("parallel", "parallel"),
        name="window_attn",
    )(qt, k, k, k, vt, vt, vt, _window_bias_table(), sink_rows, g)


def _short_conv_kernel(u_ref, gb_ref, gc_ref, up_ref, un_ref, cp_ref, cn_ref, w_ref, g_ref,
                       o_ref):
    tm = u_ref.shape[0]
    i = pl.program_id(1)
    last = pl.num_programs(1) - 1
    z = gc_ref[...] * u_ref[...]
    z_before = jnp.where(i > 0, cp_ref[SUBLANES - 1:SUBLANES, :] * up_ref[SUBLANES - 1:SUBLANES, :],
                         0.0)
    z_after = jnp.where(i < last, cn_ref[0:1, :] * un_ref[0:1, :], 0.0)
    row = lax.broadcasted_iota(jnp.int32, z.shape, 0)
    z_m1 = jnp.where(row == 0, z_before, pltpu.roll(z, 1, axis=0))
    z_p1 = jnp.where(row == tm - 1, z_after, pltpu.roll(z, tm - 1, axis=0))
    w = w_ref[...]
    y = gb_ref[...] * (w[0:1, :] * z_m1 + w[1:2, :] * z + w[2:3, :] * z_p1)
    o_ref[...] = _group_rms_norm(y, g_ref[...]).astype(o_ref.dtype)


def _short_conv(pb, conv_w, g, batch, seq_len):
    tm = min(ROW_TILE, seq_len)
    per_tile = tm // SUBLANES
    n_row_blocks = seq_len // SUBLANES
    u_col, gb_col, gc_col = 0, 1, 2

    def main(col):
        return pl.BlockSpec((None, tm, GROUP_WIDTH), lambda b, i: (b, i, col))

    def prev(col):
        return pl.BlockSpec((None, SUBLANES, GROUP_WIDTH),
                            lambda b, i: (b, jnp.maximum(i * per_tile - 1, 0), col))

    def nxt(col):
        return pl.BlockSpec((None, SUBLANES, GROUP_WIDTH),
                            lambda b, i: (b, jnp.minimum((i + 1) * per_tile, n_row_blocks - 1), col))

    return pl.pallas_call(
        _short_conv_kernel,
        grid=(batch, seq_len // tm),
        in_specs=[main(u_col), main(gb_col), main(gc_col),
                  prev(u_col), nxt(u_col), prev(gc_col), nxt(gc_col),
                  _resident((CONV_WIDTH, GROUP_WIDTH), lambda b, i: (0, 0)),
                  _resident((1, GROUP_WIDTH), lambda b, i: (0, 0))],
        out_specs=pl.BlockSpec((None, tm, GROUP_WIDTH), lambda b, i: (b, i, 0)),
        out_shape=jax.ShapeDtypeStruct((batch, seq_len, GROUP_WIDTH), BF16),
        compiler_params=_params("parallel", "parallel"),
        name="short_conv",
    )(pb, pb, pb, pb, pb, pb, pb, conv_w, g)


FOURIER_T2_PER_STEP = 8
FOURIER_K1_PER_STEP = SUBLANES


def _dft_cos_sin(n):
    idx = np.arange(n)
    ang = 2.0 * np.pi * ((idx[:, None] * idx[None, :]) % n) / n
    return np.cos(ang), np.sin(ang)


def _fourier_stage1_kernel(x_ref, wh_ref, wl_ref, tc_ref, ts_ref, o_ref):
    n1 = x_ref.shape[0]
    x_hi, x_lo = _split_bf16(x_ref[...])
    a = _dot3(wh_ref[...], wl_ref[...], x_hi, x_lo)
    a_re, a_im = a[:n1], a[n1:]
    tc, ts = tc_ref[...], ts_ref[...]
    o_ref[0] = a_re * tc + a_im * ts
    o_ref[1] = a_im * tc - a_re * ts


def _fourier_stage2_kernel(a_ref, w2h_ref, w2l_ref, w3h_ref, w3l_ref, g_ref, o_ref):
    n2 = a_ref.shape[2]
    for j in range(a_ref.shape[1]):
        a = jnp.concatenate([a_ref[0, j], a_ref[1, j]], axis=0)
        a_hi, a_lo = _split_bf16(a)
        y = _dot3(w2h_ref[...], w2l_ref[...], a_hi, a_lo)
        y2 = jnp.concatenate([y[:n2], y[n2:]], axis=1)
        y_hi, y_lo = _split_bf16(y2)
        z = _dot3(y_hi, y_lo, w3h_ref[...], w3l_ref[...])
        o_ref[:, j, :] = _group_rms_norm(z, g_ref[...]).astype(o_ref.dtype)


def _fourier(pc, g, batch, seq_len):
    n = math.isqrt(seq_len)
    assert n * n == seq_len and n % FOURIER_T2_PER_STEP == 0
    cols = FOURIER_T2_PER_STEP * GROUP_WIDTH

    cos_n, sin_n = _dft_cos_sin(n)
    w1_hi, w1_lo = _split_bf16(jnp.asarray(np.concatenate([cos_n, -sin_n], axis=0), F32))
    w2 = np.block([[cos_n, sin_n], [-sin_n, cos_n]])
    w2_hi, w2_lo = _split_bf16(jnp.asarray(w2, F32))
    cos_c, sin_c = _dft_cos_sin(HEAD_DIM)
    eye = np.eye(GROUP_WIDTH // HEAD_DIM)
    w3 = np.concatenate([np.kron(eye, cos_c), np.kron(eye, sin_c)], axis=0)
    w3_hi, w3_lo = _split_bf16(jnp.asarray(w3 / math.sqrt(seq_len * HEAD_DIM), F32))

    k1 = jnp.arange(n, dtype=jnp.int32)[:, None]
    t2 = jnp.arange(n, dtype=jnp.int32)[None, :]
    theta = (2.0 * math.pi / seq_len) * (k1 * t2).astype(F32)
    tw_cos = jnp.repeat(jnp.cos(theta), GROUP_WIDTH, axis=1)
    tw_sin = jnp.repeat(jnp.sin(theta), GROUP_WIDTH, axis=1)

    const = lambda shape: _resident(shape, lambda b, j: (0, 0))
    stage1 = pl.pallas_call(
        _fourier_stage1_kernel,
        grid=(batch, n // FOURIER_T2_PER_STEP),
        in_specs=[pl.BlockSpec((None, n, cols), lambda b, j: (b, 0, j)),
                  const((2 * n, n)), const((2 * n, n)),
                  pl.BlockSpec((n, cols), lambda b, j: (0, j)),
                  pl.BlockSpec((n, cols), lambda b, j: (0, j))],
        out_specs=pl.BlockSpec((None, 2, n, cols), lambda b, j: (b, 0, 0, j)),
        out_shape=jax.ShapeDtypeStruct((batch, 2, n, n * GROUP_WIDTH), F32),
        compiler_params=_params("parallel", "parallel"),
        name="fourier_stage1",
    )(pc.reshape(batch, n, n * GROUP_WIDTH), w1_hi, w1_lo, tw_cos, tw_sin)

    out = pl.pallas_call(
        _fourier_stage2_kernel,
        grid=(batch, n // FOURIER_K1_PER_STEP),
        in_specs=[pl.BlockSpec((None, 2, FOURIER_K1_PER_STEP, n, GROUP_WIDTH),
                               lambda b, j: (b, 0, j, 0, 0)),
                  const((2 * n, 2 * n)), const((2 * n, 2 * n)),
                  const((2 * GROUP_WIDTH, GROUP_WIDTH)), const((2 * GROUP_WIDTH, GROUP_WIDTH)),
                  const((1, GROUP_WIDTH))],
        out_specs=pl.BlockSpec((None, n, FOURIER_K1_PER_STEP, GROUP_WIDTH),
                               lambda b, j: (b, 0, j, 0)),
        out_shape=jax.ShapeDtypeStruct((batch, n, n, GROUP_WIDTH), F32),
        compiler_params=_params("parallel", "parallel"),
        name="fourier_stage2",
    )(stage1.reshape(batch, 2, n, n, GROUP_WIDTH), w2_hi, w2_lo, w3_hi, w3_lo, g)
    return out.reshape(batch, seq_len, GROUP_WIDTH)


DENSE_TQ = 512
DENSE_TK = 512


def _dense_attn_kernel(tk, unroll, qt_ref, k_ref, vt_ref, g_ref, o_ref, qpad_ref, s_ref, m_ref,
                       acc_ref):
    tq = qt_ref.shape[1]
    n_chunks = k_ref.shape[0] // tk

    m_ref[...] = jnp.full(m_ref.shape, -jnp.inf, F32)
    acc_ref[...] = jnp.zeros(acc_ref.shape, F32)
    zeros = jnp.zeros((HEAD_DIM, tq), BF16)
    for h in range(Q_HEADS):
        qh = qt_ref[h * HEAD_DIM:(h + 1) * HEAD_DIM, :]
        qpad_ref[h] = jnp.concatenate([qh, zeros] if h < GROUP else [zeros, qh], axis=0)

    def scores(j, h):
        start = pl.multiple_of(j * tk, tk)
        s = _dot(k_ref[pl.ds(start, tk), :], qpad_ref[h])
        s_ref[h % 2] = s
        return jnp.max(s, axis=0, keepdims=True)

    def softmax_pv(j, h, s_max):
        start = pl.multiple_of(j * tk, tk)
        m_old = m_ref[h:h + 1, :]
        m_new = jnp.maximum(m_old, s_max)
        p = jnp.exp2(s_ref[h % 2] - m_new).astype(BF16)
        alpha = jnp.exp2(m_old - m_new)
        kv = h // GROUP
        vt = vt_ref[kv * VT_ROWS:(kv + 1) * VT_ROWS, pl.ds(start, tk)]
        rows = slice(h * VT_ROWS, (h + 1) * VT_ROWS)
        acc_ref[rows, :] = alpha * acc_ref[rows, :] + _dot(vt, p)
        m_ref[h:h + 1, :] = m_new

    def chunk(j, s_max):
        for h in range(Q_HEADS):
            if h + 1 < Q_HEADS:
                nxt = scores(j, h + 1)
            else:
                nxt = scores(jnp.minimum(j + 1, n_chunks - 1), 0)
            softmax_pv(j, h, s_max)
            s_max = nxt
        return s_max

    lax.fori_loop(0, n_chunks, chunk, scores(0, 0), unroll=unroll)

    parts = []
    for h in range(Q_HEADS):
        row_sum = acc_ref[h * VT_ROWS + HEAD_DIM:h * VT_ROWS + HEAD_DIM + 1, :]
        parts.append(acc_ref[h * VT_ROWS:h * VT_ROWS + HEAD_DIM, :] / row_sum)
    out = jnp.concatenate(parts, axis=0).T
    o_ref[...] = _group_rms_norm(out, g_ref[...]).astype(o_ref.dtype)


def _dense_attn(qt, k, vt, g, batch, seq_len, tk=DENSE_TK, unroll=1):
    tq = min(DENSE_TQ, seq_len)
    tk = min(tk, seq_len)
    return pl.pallas_call(
        functools.partial(_dense_attn_kernel, tk, unroll),
        grid=(batch, seq_len // tq),
        in_specs=[pl.BlockSpec((None, GROUP_WIDTH, tq), lambda b, i: (b, 0, i)),
                  pl.BlockSpec((None, seq_len, KV_WIDTH), lambda b, i: (b, 0, 0)),
                  pl.BlockSpec((None, KV_HEADS * VT_ROWS, seq_len), lambda b, i: (b, 0, 0)),
                  _resident((1, GROUP_WIDTH), lambda b, i: (0, 0))],
        out_specs=pl.BlockSpec((None, tq, GROUP_WIDTH), lambda b, i: (b, i, 0)),
        out_shape=jax.ShapeDtypeStruct((batch, seq_len, GROUP_WIDTH), BF16),
        scratch_shapes=[pltpu.VMEM((Q_HEADS, KV_WIDTH, tq), BF16),
                        pltpu.VMEM((2, tk, tq), F32),
                        pltpu.VMEM((Q_HEADS, tq), F32),
                        pltpu.VMEM((Q_HEADS * VT_ROWS, tq), F32)],
        compiler_params=_params("parallel", "parallel"),
        name="dense_attn",
    )(qt, k, vt, g)


FF_CHUNK = 1024


def _out_mlp_kernel(ma_ref, mb_ref, mc_ref, md_ref, h_ref, wo_ref, g1_ref, b1_ref,
                    w1_ref, w2_ref, g2_ref, b2_ref, o_ref):
    mix = jnp.concatenate([r[...].astype(BF16) for r in (ma_ref, mb_ref, mc_ref, md_ref)], axis=-1)
    h1 = _layer_norm(DN_ALPHA * h_ref[...] + _dot(mix, wo_ref[...]), g1_ref[...], b1_ref[...])
    h1_bf = h1.astype(BF16)
    ffn = jnp.zeros(h1.shape, F32)
    for c in range(D_FF // FF_CHUNK):
        cols = slice(c * FF_CHUNK, (c + 1) * FF_CHUNK)
        u = jnp.maximum(_dot(h1_bf, w1_ref[:, cols]), 0.0)
        ffn = ffn + _dot((u * u).astype(BF16), w2_ref[cols, :])
    o_ref[...] = _layer_norm(DN_ALPHA * h1 + ffn, g2_ref[...], b2_ref[...])


def _out_mlp(mixes, h, w_out, g1, b1, w1, w2, g2, b2):
    rows = h.shape[0]
    tm = min(ROW_TILE, rows)
    row_spec = lambda width: pl.BlockSpec((tm, width), lambda i: (i, 0))
    const = lambda shape: _resident(shape, lambda i: (0, 0))
    vec = const((1, D_MODEL))
    return pl.pallas_call(
        _out_mlp_kernel,
        grid=(rows // tm,),
        in_specs=[row_spec(GROUP_WIDTH)] * 4 + [row_spec(D_MODEL), const((D_MODEL, D_MODEL)),
                                                 vec, vec, const((D_MODEL, D_FF)),
                                                 const((D_FF, D_MODEL)), vec, vec],
        out_specs=row_spec(D_MODEL),
        out_shape=jax.ShapeDtypeStruct((rows, D_MODEL), F32),
        compiler_params=_params("parallel"),
        name="out_mlp",
    )(*mixes, h, w_out, g1, b1, w1, w2, g2, b2)


def kernel(x, ln_in_g, ln_in_b, w_in, conv_w, sink, qn_g, kn_g, grp_g, w_out, ln1_g, ln1_b,
           w1, w2, ln2_g, ln2_b):
    batch, seq_len, _ = x.shape
    rows = batch * seq_len
    depth = w_in.shape[0]
    row_vec = lambda v: v.reshape(1, -1)

    h = x
    for l in range(depth):
        first = l == 0
        outs = _in_proj(h, row_vec(ln_in_g), row_vec(ln_in_b), w_in[l].astype(BF16), qn_g[l],
                        kn_g[l], first, batch, seq_len)
        if first:
            h, *outs = outs
        qta, ka, vta, pb, pc, qtd, kd, vtd = outs
        g = [row_vec(grp_g[l, i * GROUP_WIDTH:(i + 1) * GROUP_WIDTH]) for i in range(4)]
        mix_a = _window_attn(qta, ka, vta, sink[l], g[0], batch, seq_len)
        mix_b = _short_conv(pb, conv_w[l], g[1], batch, seq_len)
        mix_c = _fourier(pc, g[2], batch, seq_len)
        mix_d = _dense_attn(qtd, kd, vtd, g[3], batch, seq_len, tk=DENSE_TK * (l + 1))
        mixes = [m.reshape(rows, GROUP_WIDTH) for m in (mix_a, mix_b, mix_c, mix_d)]
        h = _out_mlp(mixes, h.reshape(rows, D_MODEL), w_out[l].astype(BF16), row_vec(ln1_g[l]),
                     row_vec(ln1_b[l]), w1[l].astype(BF16), w2[l].astype(BF16), row_vec(ln2_g[l]),
                     row_vec(ln2_b[l])).reshape(batch, seq_len, D_MODEL)
    return h
```

```python
import functools
import math

import numpy as np
import jax
import jax.numpy as jnp
from jax import lax
from jax.experimental import pallas as pl
from jax.experimental.pallas import tpu as pltpu

F32 = jnp.float32
BF16 = jnp.bfloat16

D_MODEL = 1024
HEAD_DIM = 64
GROUP_WIDTH = 256
Q_HEADS = 4
KV_HEADS = 2
GROUP = Q_HEADS // KV_HEADS
KV_WIDTH = KV_HEADS * HEAD_DIM
CONV_WIDTH = 3
WINDOW = 128
BLOCK = 128
GRID_W = 64
AXIS_DIM = HEAD_DIM // 2
ROPE_THETA = 10000.0
D_FF = 4 * D_MODEL
LN_EPS = 1e-5
RMS_EPS = 1e-6
DEPTH = 2
DN_ALPHA = (2 * DEPTH) ** 0.25
NEG_INF = -1e30
LOG2_E = math.log2(math.e)
ALIBI_SLOPES = tuple(2.0 ** (-8.0 * (h + 1) / Q_HEADS) for h in range(Q_HEADS))

V7X_VMEM_LIMIT_BYTES = 56 * 1024 * 1024
LANES = 128
SUBLANES = 8
BF16_SUBLANES = 16
VT_ROWS = HEAD_DIM + BF16_SUBLANES
ROW_TILE = 512


def _params(*semantics):
    return pltpu.CompilerParams(dimension_semantics=semantics,
                                vmem_limit_bytes=V7X_VMEM_LIMIT_BYTES)


def _resident(block_shape, index_map):
    return pl.BlockSpec(block_shape, index_map, pipeline_mode=pl.Buffered(1))


def _dot(a, b):
    return jnp.dot(a, b, preferred_element_type=F32)


def _split_bf16(x):
    hi = x.astype(BF16)
    lo = (x - hi.astype(F32)).astype(BF16)
    return hi, lo


def _dot3(a_hi, a_lo, b_hi, b_lo):
    return _dot(a_hi, b_hi) + _dot(a_hi, b_lo) + _dot(a_lo, b_hi)


def _layer_norm(x, g, b):
    mu = jnp.mean(x, axis=-1, keepdims=True)
    xc = x - mu
    var = jnp.mean(xc * xc, axis=-1, keepdims=True)
    return xc * lax.rsqrt(var + LN_EPS) * g + b


def _group_rms_norm(x, g):
    ms = jnp.mean(x * x, axis=-1, keepdims=True)
    return x * lax.rsqrt(ms + RMS_EPS) * g


A_COLS = 512
B_COLS = 768
C_COLS = 256
D_COLS = 512
D_IN_PROJ = A_COLS + B_COLS + C_COLS + D_COLS


def _rope_tables(seq_len):
    pos = jnp.arange(seq_len)
    inv_freq = ROPE_THETA ** (-jnp.arange(0, AXIS_DIM, 2, dtype=F32) / AXIS_DIM)
    ang_row = (pos // GRID_W).astype(F32)[:, None] * inv_freq[None, :]
    ang_col = (pos % GRID_W).astype(F32)[:, None] * inv_freq[None, :]
    ang = jnp.concatenate([ang_row, ang_row, ang_col, ang_col], axis=-1)
    half = AXIS_DIM // 2
    sign = np.tile(np.concatenate([-np.ones(half), np.ones(half)]), HEAD_DIM // AXIS_DIM)
    cos = jnp.tile(jnp.cos(ang), (1, LANES // HEAD_DIM))
    sin_signed = jnp.tile(jnp.sin(ang) * jnp.asarray(sign, F32), (1, LANES // HEAD_DIM))
    return cos, sin_signed


def _head_norm_rope(x, gain, cos, sin_signed, head_mean):
    width = x.shape[-1]
    sq_hi, sq_lo = _split_bf16(x * x)
    ms = _dot(sq_hi, head_mean) + _dot(sq_lo, head_mean)
    xn = x * lax.rsqrt(ms + RMS_EPS) * gain
    reps = width // LANES
    if reps > 1:
        cos = jnp.concatenate([cos] * reps, axis=-1)
        sin_signed = jnp.concatenate([sin_signed] * reps, axis=-1)
    half = AXIS_DIM // 2
    lane = lax.broadcasted_iota(jnp.int32, x.shape, 1)
    first_half = (lane % AXIS_DIM) < half
    partner = jnp.where(first_half, pltpu.roll(xn, width - half, axis=1),
                        pltpu.roll(xn, half, axis=1))
    return xn * cos + partner * sin_signed


def _store_attn_operands(q, k, v, qt_ref, k_ref, vt_ref):
    rows = q.shape[0]
    qt_ref[...] = (q * (HEAD_DIM ** -0.5 * LOG2_E)).T.astype(BF16)
    k_ref[...] = k.astype(BF16)
    vt = v.T.astype(BF16)
    pad = (lax.broadcasted_iota(jnp.int32, (VT_ROWS - HEAD_DIM, rows), 0) == 0).astype(BF16)
    vt_ref[...] = jnp.concatenate(
        [piece for kv in range(KV_HEADS)
         for piece in (vt[kv * HEAD_DIM:(kv + 1) * HEAD_DIM], pad)], axis=0)


def _in_proj_kernel(apply_ln, x_ref, g_ref, b_ref, w_ref, cos_ref, sin_ref, gq_ref, gk_ref,
                    hm_ref, *out_refs):
    x = x_ref[...]
    if apply_ln:
        h_ref, *out_refs = out_refs
        x = _layer_norm(x, g_ref[...], b_ref[...])
        h_ref[...] = x
    qta_ref, ka_ref, vta_ref, pb_ref, pc_ref, qtd_ref, kd_ref, vtd_ref = out_refs
    proj = _dot(x.astype(BF16), w_ref[...])
    a_q, a_k, a_v = (proj[:, :GROUP_WIDTH], proj[:, GROUP_WIDTH:GROUP_WIDTH + KV_WIDTH],
                     proj[:, GROUP_WIDTH + KV_WIDTH:A_COLS])
    _store_attn_operands(a_q, a_k, a_v, qta_ref, ka_ref, vta_ref)
    pb_ref[...] = proj[:, A_COLS:A_COLS + B_COLS]
    pc_ref[...] = proj[:, A_COLS + B_COLS:A_COLS + B_COLS + C_COLS]
    d0 = A_COLS + B_COLS + C_COLS
    d_q, d_k, d_v = (proj[:, d0:d0 + GROUP_WIDTH],
                     proj[:, d0 + GROUP_WIDTH:d0 + GROUP_WIDTH + KV_WIDTH],
                     proj[:, d0 + GROUP_WIDTH + KV_WIDTH:])
    cos, sin_signed, hm = cos_ref[...], sin_ref[...], hm_ref[...]
    d_q = _head_norm_rope(d_q, gq_ref[...], cos, sin_signed, hm)
    d_k = _head_norm_rope(d_k, gk_ref[...], cos, sin_signed, hm[:KV_WIDTH, :KV_WIDTH])
    _store_attn_operands(d_q, d_k, d_v, qtd_ref, kd_ref, vtd_ref)


def _in_proj(x, g, b, w, qn_g, kn_g, apply_ln, batch, seq_len):
    tm = min(ROW_TILE, seq_len)
    cos, sin_signed = _rope_tables(seq_len)
    gq = jnp.tile(qn_g, GROUP_WIDTH // HEAD_DIM).reshape(1, GROUP_WIDTH)
    gk = jnp.tile(kn_g, KV_WIDTH // HEAD_DIM).reshape(1, KV_WIDTH)
    head_mean = jnp.asarray(
        np.kron(np.eye(GROUP_WIDTH // HEAD_DIM), np.full((HEAD_DIM, HEAD_DIM), 1.0 / HEAD_DIM)),
        BF16)
    rows_spec = lambda width: pl.BlockSpec((None, tm, width), lambda bi, i: (bi, i, 0))
    cols_spec = lambda height: pl.BlockSpec((None, height, tm), lambda bi, i: (bi, 0, i))
    table_spec = pl.BlockSpec((tm, LANES), lambda bi, i: (i, 0))
    const = lambda shape: _resident(shape, lambda bi, i: (0, 0))
    rows_shape = lambda width, dt: jax.ShapeDtypeStruct((batch, seq_len, width), dt)
    cols_shape = lambda height: jax.ShapeDtypeStruct((batch, height, seq_len), BF16)
    attn_specs = [cols_spec(GROUP_WIDTH), rows_spec(KV_WIDTH), cols_spec(KV_HEADS * VT_ROWS)]
    attn_shapes = [cols_shape(GROUP_WIDTH), rows_shape(KV_WIDTH, BF16),
                   cols_shape(KV_HEADS * VT_ROWS)]
    out_specs = attn_specs + [rows_spec(B_COLS), rows_spec(C_COLS)] + attn_specs
    out_shape = attn_shapes + [rows_shape(B_COLS, F32), rows_shape(C_COLS, F32)] + attn_shapes
    if apply_ln:
        out_specs = [rows_spec(D_MODEL)] + out_specs
        out_shape = [rows_shape(D_MODEL, F32)] + out_shape
    return pl.pallas_call(
        functools.partial(_in_proj_kernel, apply_ln),
        grid=(batch, seq_len // tm),
        in_specs=[rows_spec(D_MODEL), const((1, D_MODEL)), const((1, D_MODEL)),
                  const((D_MODEL, D_IN_PROJ)), table_spec, table_spec,
                  const((1, GROUP_WIDTH)), const((1, KV_WIDTH)), const((GROUP_WIDTH, GROUP_WIDTH))],
        out_specs=out_specs,
        out_shape=out_shape,
        compiler_params=_params("parallel", "parallel"),
        name="in_proj_ln" if apply_ln else "in_proj",
    )(x, g, b, w, cos, sin_signed, gq, gk, head_mean)


WINDOW_TQ = 2048


def _window_bias_table():
    key = np.arange(3 * BLOCK)[:, None]
    query = np.arange(BLOCK)[None, :]
    dist = np.abs(BLOCK + query - key)
    per_head = [np.where(dist <= WINDOW, -slope * LOG2_E * dist, NEG_INF) for slope in ALIBI_SLOPES]
    groups = [np.concatenate(per_head[kv * GROUP:(kv + 1) * GROUP], axis=1) for kv in range(KV_HEADS)]
    return jnp.asarray(np.stack(groups), F32)


def _window_attn_kernel(qt_ref, km_ref, kp_ref, kn_ref, vm_ref, vp_ref, vn_ref, bias_ref,
                        sink_ref, g_ref, o_ref, kall_ref, vall_ref, s_ref):
    tq = qt_ref.shape[1]
    n_blk = tq // BLOCK
    i = pl.program_id(1)
    before_start = jnp.where(i == 0, NEG_INF, 0.0)
    after_end = jnp.where(i == pl.num_programs(1) - 1, NEG_INF, 0.0)

    kall_ref[0:BLOCK, :] = kp_ref[...]
    kall_ref[BLOCK:BLOCK + tq, :] = km_ref[...]
    kall_ref[BLOCK + tq:, :] = kn_ref[...]
    vall_ref[:, 0:BLOCK] = vp_ref[...]
    vall_ref[:, BLOCK:BLOCK + tq] = vm_ref[...]
    vall_ref[:, BLOCK + tq:] = vn_ref[...]
    zeros = jnp.zeros((HEAD_DIM, BLOCK), BF16)
    g = g_ref[...]

    def scores(jb, kv):
        cols = slice(jb * BLOCK, (jb + 1) * BLOCK)
        halves = []
        for h in range(kv * GROUP, (kv + 1) * GROUP):
            qh = qt_ref[h * HEAD_DIM:(h + 1) * HEAD_DIM, cols]
            halves.append(jnp.concatenate([qh, zeros] if kv == 0 else [zeros, qh], axis=0))
        kw = kall_ref[jb * BLOCK:(jb + 3) * BLOCK, :]
        s = _dot(kw, jnp.concatenate(halves, axis=1)) + bias_ref[kv]
        if jb == 0:
            s = jnp.concatenate([s[:BLOCK] + before_start, s[BLOCK:]], axis=0)
        if jb == n_blk - 1:
            s = jnp.concatenate([s[:2 * BLOCK], s[2 * BLOCK:] + after_end], axis=0)
        s_ref[kv] = s
        return jnp.max(s, axis=0, keepdims=True)

    def softmax_pv(jb, kv, s_max):
        sink = sink_ref[kv]
        m = jnp.maximum(s_max, sink)
        p = jnp.exp2(s_ref[kv] - m).astype(BF16)
        vt = vall_ref[kv * VT_ROWS:(kv + 1) * VT_ROWS, jb * BLOCK:(jb + 3) * BLOCK]
        return _dot(vt, p), jnp.exp2(sink - m)

    outs = []

    def finish(jb, kv, pv, sink_term):
        o = pv[:HEAD_DIM] / (pv[HEAD_DIM:HEAD_DIM + 1] + sink_term)
        outs.extend([o[:, :BLOCK], o[:, BLOCK:]])
        if kv == KV_HEADS - 1:
            out = jnp.concatenate(outs, axis=0).T
            o_ref[jb * BLOCK:(jb + 1) * BLOCK, :] = _group_rms_norm(out, g).astype(o_ref.dtype)
            outs.clear()

    units = [(jb, kv) for jb in range(n_blk) for kv in range(KV_HEADS)]
    s_max = scores(*units[0])
    pending = None
    for idx, unit in enumerate(units):
        nxt = scores(*units[idx + 1]) if idx + 1 < len(units) else None
        pv_parts = softmax_pv(*unit, s_max)
        if pending is not None:
            finish(*pending)
        pending = (*unit, *pv_parts)
        s_max = nxt
    finish(*pending)


def _window_attn(qt, k, vt, sink, g, batch, seq_len):
    tq = min(WINDOW_TQ, seq_len)
    per_tile = tq // BLOCK
    n_blocks = seq_len // BLOCK
    sink_rows = jnp.repeat(sink * LOG2_E, BLOCK).reshape(KV_HEADS, 1, GROUP * BLOCK)
    prev_idx = lambda i: jnp.maximum(i * per_tile - 1, 0)
    next_idx = lambda i: jnp.minimum((i + 1) * per_tile, n_blocks - 1)
    v_rows = KV_HEADS * VT_ROWS
    const3 = lambda shape: _resident(shape, lambda b, i: (0, 0, 0))
    return pl.pallas_call(
        _window_attn_kernel,
        grid=(batch, seq_len // tq),
        in_specs=[pl.BlockSpec((None, GROUP_WIDTH, tq), lambda b, i: (b, 0, i)),
                  pl.BlockSpec((None, tq, KV_WIDTH), lambda b, i: (b, i, 0)),
                  pl.BlockSpec((None, BLOCK, KV_WIDTH), lambda b, i: (b, prev_idx(i), 0)),
                  pl.BlockSpec((None, BLOCK, KV_WIDTH), lambda b, i: (b, next_idx(i), 0)),
                  pl.BlockSpec((None, v_rows, tq), lambda b, i: (b, 0, i)),
                  pl.BlockSpec((None, v_rows, BLOCK), lambda b, i: (b, 0, prev_idx(i))),
                  pl.BlockSpec((None, v_rows, BLOCK), lambda b, i: (b, 0, next_idx(i))),
                  const3((KV_HEADS, 3 * BLOCK, GROUP * BLOCK)),
                  const3((KV_HEADS, 1, GROUP * BLOCK)),
                  _resident((1, GROUP_WIDTH), lambda b, i: (0, 0))],
        out_specs=pl.BlockSpec((None, tq, GROUP_WIDTH), lambda b, i: (b, i, 0)),
        out_shape=jax.ShapeDtypeStruct((batch, seq_len, GROUP_WIDTH), BF16),
        scratch_shapes=[pltpu.VMEM((tq + 2 * BLOCK, KV_WIDTH), BF16),
                        pltpu.VMEM((v_rows, tq + 2 * BLOCK), BF16),
                        pltpu.VMEM((KV_HEADS, 3 * BLOCK, GROUP * BLOCK), F32)],
        compiler_params=_params("parallel", "parallel"),
        name="window_attn",
    )(qt, k, k, k, vt, vt, vt, _window_bias_table(), sink_rows, g)


def _short_conv_kernel(u_ref, gb_ref, gc_ref, up_ref, un_ref, cp_ref, cn_ref, w_ref, g_ref,
                       o_ref):
    tm = u_ref.shape[0]
    i = pl.program_id(1)
    last = pl.num_programs(1) - 1
    z = gc_ref[...] * u_ref[...]
    z_before = jnp.where(i > 0, cp_ref[SUBLANES - 1:SUBLANES, :] * up_ref[SUBLANES - 1:SUBLANES, :],
                         0.0)
    z_after = jnp.where(i < last, cn_ref[0:1, :] * un_ref[0:1, :], 0.0)
    row = lax.broadcasted_iota(jnp.int32, z.shape, 0)
    z_m1 = jnp.where(row == 0, z_before, pltpu.roll(z, 1, axis=0))
    z_p1 = jnp.where(row == tm - 1, z_after, pltpu.roll(z, tm - 1, axis=0))
    w = w_ref[...]
    y = gb_ref[...] * (w[0:1, :] * z_m1 + w[1:2, :] * z + w[2:3, :] * z_p1)
    o_ref[...] = _group_rms_norm(y, g_ref[...]).astype(o_ref.dtype)


def _short_conv(pb, conv_w, g, batch, seq_len):
    tm = min(ROW_TILE, seq_len)
    per_tile = tm // SUBLANES
    n_row_blocks = seq_len // SUBLANES
    u_col, gb_col, gc_col = 0, 1, 2

    def main(col):
        return pl.BlockSpec((None, tm, GROUP_WIDTH), lambda b, i: (b, i, col))

    def prev(col):
        return pl.BlockSpec((None, SUBLANES, GROUP_WIDTH),
                            lambda b, i: (b, jnp.maximum(i * per_tile - 1, 0), col))

    def nxt(col):
        return pl.BlockSpec((None, SUBLANES, GROUP_WIDTH),
                            lambda b, i: (b, jnp.minimum((i + 1) * per_tile, n_row_blocks - 1), col))

    return pl.pallas_call(
        _short_conv_kernel,
        grid=(batch, seq_len // tm),
        in_specs=[main(u_col), main(gb_col), main(gc_col),
                  prev(u_col), nxt(u_col), prev(gc_col), nxt(gc_col),
                  _resident((CONV_WIDTH, GROUP_WIDTH), lambda b, i: (0, 0)),
                  _resident((1, GROUP_WIDTH), lambda b, i: (0, 0))],
        out_specs=pl.BlockSpec((None, tm, GROUP_WIDTH), lambda b, i: (b, i, 0)),
        out_shape=jax.ShapeDtypeStruct((batch, seq_len, GROUP_WIDTH), BF16),
        compiler_params=_params("parallel", "parallel"),
        name="short_conv",
    )(pb, pb, pb, pb, pb, pb, pb, conv_w, g)


FOURIER_T2_PER_STEP = 8
FOURIER_K1_PER_STEP = SUBLANES


def _dft_cos_sin(n):
    idx = np.arange(n)
    ang = 2.0 * np.pi * ((idx[:, None] * idx[None, :]) % n) / n
    return np.cos(ang), np.sin(ang)


def _fourier_stage1_kernel(x_ref, wh_ref, wl_ref, tc_ref, ts_ref, o_ref):
    n1 = x_ref.shape[0]
    x_hi, x_lo = _split_bf16(x_ref[...])
    a = _dot3(wh_ref[...], wl_ref[...], x_hi, x_lo)
    a_re, a_im = a[:n1], a[n1:]
    tc, ts = tc_ref[...], ts_ref[...]
    o_ref[0] = a_re * tc + a_im * ts
    o_ref[1] = a_im * tc - a_re * ts


def _fourier_stage2_kernel(a_ref, w2h_ref, w2l_ref, w3h_ref, w3l_ref, g_ref, o_ref):
    n2 = a_ref.shape[2]
    for j in range(a_ref.shape[1]):
        a = jnp.concatenate([a_ref[0, j], a_ref[1, j]], axis=0)
        a_hi, a_lo = _split_bf16(a)
        y = _dot3(w2h_ref[...], w2l_ref[...], a_hi, a_lo)
        y2 = jnp.concatenate([y[:n2], y[n2:]], axis=1)
        y_hi, y_lo = _split_bf16(y2)
        z = _dot3(y_hi, y_lo, w3h_ref[...], w3l_ref[...])
        o_ref[:, j, :] = _group_rms_norm(z, g_ref[...]).astype(o_ref.dtype)


def _fourier(pc, g, batch, seq_len):
    n = math.isqrt(seq_len)
    assert n * n == seq_len and n % FOURIER_T2_PER_STEP == 0
    cols = FOURIER_T2_PER_STEP * GROUP_WIDTH

    cos_n, sin_n = _dft_cos_sin(n)
    w1_hi, w1_lo = _split_bf16(jnp.asarray(np.concatenate([cos_n, -sin_n], axis=0), F32))
    w2 = np.block([[cos_n, sin_n], [-sin_n, cos_n]])
    w2_hi, w2_lo = _split_bf16(jnp.asarray(w2, F32))
    cos_c, sin_c = _dft_cos_sin(HEAD_DIM)
    eye = np.eye(GROUP_WIDTH // HEAD_DIM)
    w3 = np.concatenate([np.kron(eye, cos_c), np.kron(eye, sin_c)], axis=0)
    w3_hi, w3_lo = _split_bf16(jnp.asarray(w3 / math.sqrt(seq_len * HEAD_DIM), F32))

    k1 = jnp.arange(n, dtype=jnp.int32)[:, None]
    t2 = jnp.arange(n, dtype=jnp.int32)[None, :]
    theta = (2.0 * math.pi / seq_len) * (k1 * t2).astype(F32)
    tw_cos = jnp.repeat(jnp.cos(theta), GROUP_WIDTH, axis=1)
    tw_sin = jnp.repeat(jnp.sin(theta), GROUP_WIDTH, axis=1)

    const = lambda shape: _resident(shape, lambda b, j: (0, 0))
    stage1 = pl.pallas_call(
        _fourier_stage1_kernel,
        grid=(batch, n // FOURIER_T2_PER_STEP),
        in_specs=[pl.BlockSpec((None, n, cols), lambda b, j: (b, 0, j)),
                  const((2 * n, n)), const((2 * n, n)),
                  pl.BlockSpec((n, cols), lambda b, j: (0, j)),
                  pl.BlockSpec((n, cols), lambda b, j: (0, j))],
        out_specs=pl.BlockSpec((None, 2, n, cols), lambda b, j: (b, 0, 0, j)),
        out_shape=jax.ShapeDtypeStruct((batch, 2, n, n * GROUP_WIDTH), F32),
        compiler_params=_params("parallel", "parallel"),
        name="fourier_stage1",
    )(pc.reshape(batch, n, n * GROUP_WIDTH), w1_hi, w1_lo, tw_cos, tw_sin)

    out = pl.pallas_call(
        _fourier_stage2_kernel,
        grid=(batch, n // FOURIER_K1_PER_STEP),
        in_specs=[pl.BlockSpec((None, 2, FOURIER_K1_PER_STEP, n, GROUP_WIDTH),
                               lambda b, j: (b, 0, j, 0, 0)),
                  const((2 * n, 2 * n)), const((2 * n, 2 * n)),
                  const((2 * GROUP_WIDTH, GROUP_WIDTH)), const((2 * GROUP_WIDTH, GROUP_WIDTH)),
                  const((1, GROUP_WIDTH))],
        out_specs=pl.BlockSpec((None, n, FOURIER_K1_PER_STEP, GROUP_WIDTH),
                               lambda b, j: (b, 0, j, 0)),
        out_shape=jax.ShapeDtypeStruct((batch, n, n, GROUP_WIDTH), F32),
        compiler_params=_params("parallel", "parallel"),
        name="fourier_stage2",
    )(stage1.reshape(batch, 2, n, n, GROUP_WIDTH), w2_hi, w2_lo, w3_hi, w3_lo, g)
    return out.reshape(batch, seq_len, GROUP_WIDTH)


DENSE_TQ = 512
DENSE_TK = 512


def _dense_attn_kernel(tk, unroll, qt_ref, k_ref, vt_ref, g_ref, o_ref, qpad_ref, s_ref, m_ref,
                       acc_ref):
    tq = qt_ref.shape[1]
    n_chunks = k_ref.shape[0] // tk

    m_ref[...] = jnp.full(m_ref.shape, -jnp.inf, F32)
    acc_ref[...] = jnp.zeros(acc_ref.shape, F32)
    zeros = jnp.zeros((HEAD_DIM, tq), BF16)
    for h in range(Q_HEADS):
        qh = qt_ref[h * HEAD_DIM:(h + 1) * HEAD_DIM, :]
        qpad_ref[h] = jnp.concatenate([qh, zeros] if h < GROUP else [zeros, qh], axis=0)

    def scores(j, h):
        start = pl.multiple_of(j * tk, tk)
        s = _dot(k_ref[pl.ds(start, tk), :], qpad_ref[h])
        s_ref[h % 2] = s
        return jnp.max(s, axis=0, keepdims=True)

    def softmax_pv(j, h, s_max):
        start = pl.multiple_of(j * tk, tk)
        m_old = m_ref[h:h + 1, :]
        m_new = jnp.maximum(m_old, s_max)
        p = jnp.exp2(s_ref[h % 2] - m_new).astype(BF16)
        alpha = jnp.exp2(m_old - m_new)
        kv = h // GROUP
        vt = vt_ref[kv * VT_ROWS:(kv + 1) * VT_ROWS, pl.ds(start, tk)]
        rows = slice(h * VT_ROWS, (h + 1) * VT_ROWS)
        acc_ref[rows, :] = alpha * acc_ref[rows, :] + _dot(vt, p)
        m_ref[h:h + 1, :] = m_new

    def chunk(j, s_max):
        for h in range(Q_HEADS):
            if h + 1 < Q_HEADS:
                nxt = scores(j, h + 1)
            else:
                nxt = scores(jnp.minimum(j + 1, n_chunks - 1), 0)
            softmax_pv(j, h, s_max)
            s_max = nxt
        return s_max

    lax.fori_loop(0, n_chunks, chunk, scores(0, 0), unroll=unroll)

    parts = []
    for h in range(Q_HEADS):
        row_sum = acc_ref[h * VT_ROWS + HEAD_DIM:h * VT_ROWS + HEAD_DIM + 1, :]
        parts.append(acc_ref[h * VT_ROWS:h * VT_ROWS + HEAD_DIM, :] / row_sum)
    out = jnp.concatenate(parts, axis=0).T
    o_ref[...] = _group_rms_norm(out, g_ref[...]).astype(o_ref.dtype)


def _dense_attn(qt, k, vt, g, batch, seq_len, tk=DENSE_TK, unroll=1):
    tq = min(DENSE_TQ, seq_len)
    tk = min(tk, seq_len)
    return pl.pallas_call(
        functools.partial(_dense_attn_kernel, tk, unroll),
        grid=(batch, seq_len // tq),
        in_specs=[pl.BlockSpec((None, GROUP_WIDTH, tq), lambda b, i: (b, 0, i)),
                  pl.BlockSpec((None, seq_len, KV_WIDTH), lambda b, i: (b, 0, 0)),
                  pl.BlockSpec((None, KV_HEADS * VT_ROWS, seq_len), lambda b, i: (b, 0, 0)),
                  _resident((1, GROUP_WIDTH), lambda b, i: (0, 0))],
        out_specs=pl.BlockSpec((None, tq, GROUP_WIDTH), lambda b, i: (b, i, 0)),
        out_shape=jax.ShapeDtypeStruct((batch, seq_len, GROUP_WIDTH), BF16),
        scratch_shapes=[pltpu.VMEM((Q_HEADS, KV_WIDTH, tq), BF16),
                        pltpu.VMEM((2, tk, tq), F32),
                        pltpu.VMEM((Q_HEADS, tq), F32),
                        pltpu.VMEM((Q_HEADS * VT_ROWS, tq), F32)],
        compiler_params=_params("parallel", "parallel"),
        name="dense_attn",
    )(qt, k, vt, g)


FF_CHUNK = 1024


def _out_mlp_kernel(ma_ref, mb_ref, mc_ref, md_ref, h_ref, wo_ref, g1_ref, b1_ref,
                    w1_ref, w2_ref, g2_ref, b2_ref, o_ref):
    mix = jnp.concatenate([r[...].astype(BF16) for r in (ma_ref, mb_ref, mc_ref, md_ref)], axis=-1)
    h1 = _layer_norm(DN_ALPHA * h_ref[...] + _dot(mix, wo_ref[...]), g1_ref[...], b1_ref[...])
    h1_bf = h1.astype(BF16)
    ffn = jnp.zeros(h1.shape, F32)
    for c in range(D_FF // FF_CHUNK):
        cols = slice(c * FF_CHUNK, (c + 1) * FF_CHUNK)
        u = jnp.maximum(_dot(h1_bf, w1_ref[:, cols]), 0.0)
        ffn = ffn + _dot((u * u).astype(BF16), w2_ref[cols, :])
    o_ref[...] = _layer_norm(DN_ALPHA * h1 + ffn, g2_ref[...], b2_ref[...])


def _out_mlp(mixes, h, w_out, g1, b1, w1, w2, g2, b2):
    rows = h.shape[0]
    tm = min(ROW_TILE, rows)
    row_spec = lambda width: pl.BlockSpec((tm, width), lambda i: (i, 0))
    const = lambda shape: _resident(shape, lambda i: (0, 0))
    vec = const((1, D_MODEL))
    return pl.pallas_call(
        _out_mlp_kernel,
        grid=(rows // tm,),
        in_specs=[row_spec(GROUP_WIDTH)] * 4 + [row_spec(D_MODEL), const((D_MODEL, D_MODEL)),
                                                 vec, vec, const((D_MODEL, D_FF)),
                                                 const((D_FF, D_MODEL)), vec, vec],
        out_specs=row_spec(D_MODEL),
        out_shape=jax.ShapeDtypeStruct((rows, D_MODEL), F32),
        compiler_params=_params("parallel"),
        name="out_mlp",
    )(*mixes, h, w_out, g1, b1, w1, w2, g2, b2)


def kernel(x, ln_in_g, ln_in_b, w_in, conv_w, sink, qn_g, kn_g, grp_g, w_out, ln1_g, ln1_b,
           w1, w2, ln2_g, ln2_b):
    batch, seq_len, _ = x.shape
    rows = batch * seq_len
    depth = w_in.shape[0]
    row_vec = lambda v: v.reshape(1, -1)

    h = x
    for l in range(depth):
        first = l == 0
        outs = _in_proj(h, row_vec(ln_in_g), row_vec(ln_in_b), w_in[l].astype(BF16), qn_g[l],
                        kn_g[l], first, batch, seq_len)
        if first:
            h, *outs = outs
        qta, ka, vta, pb, pc, qtd, kd, vtd = outs
        g = [row_vec(grp_g[l, i * GROUP_WIDTH:(i + 1) * GROUP_WIDTH]) for i in range(4)]
        mix_a = _window_attn(qta, ka, vta, sink[l], g[0], batch, seq_len)
        mix_b = _short_conv(pb, conv_w[l], g[1], batch, seq_len)
        mix_c = _fourier(pc, g[2], batch, seq_len)
        mix_d = _dense_attn(qtd, kd, vtd, g[3], batch, seq_len, tk=2 * DENSE_TK * (l + 1),
                            unroll=2 - l)
        mixes = [m.reshape(rows, GROUP_WIDTH) for m in (mix_a, mix_b, mix_c, mix_d)]
        h = _out_mlp(mixes, h.reshape(rows, D_MODEL), w_out[l].astype(BF16), row_vec(ln1_g[l]),
                     row_vec(ln1_b[l]), w1[l].astype(BF16), w2[l].astype(BF16), row_vec(ln2_g[l]),
                     row_vec(ln2_b[l])).reshape(batch, seq_len, D_MODEL)
    return h
```

```python
import functools
import math

import numpy as np
import jax
import jax.numpy as jnp
from jax import lax
from jax.experimental import pallas as pl
from jax.experimental.pallas import tpu as pltpu

F32 = jnp.float32
BF16 = jnp.bfloat16

D_MODEL = 1024
HEAD_DIM = 64
GROUP_WIDTH = 256
Q_HEADS = 4
KV_HEADS = 2
GROUP = Q_HEADS // KV_HEADS
KV_WIDTH = KV_HEADS * HEAD_DIM
CONV_WIDTH = 3
WINDOW = 128
BLOCK = 128
GRID_W = 64
AXIS_DIM = HEAD_DIM // 2
ROPE_THETA = 10000.0
D_FF = 4 * D_MODEL
LN_EPS = 1e-5
RMS_EPS = 1e-6
DEPTH = 2
DN_ALPHA = (2 * DEPTH) ** 0.25
NEG_INF = -1e30
LOG2_E = math.log2(math.e)
ALIBI_SLOPES = tuple(2.0 ** (-8.0 * (h + 1) / Q_HEADS) for h in range(Q_HEADS))

V7X_VMEM_LIMIT_BYTES = 56 * 1024 * 1024
LANES = 128
SUBLANES = 8
BF16_SUBLANES = 16
VT_ROWS = HEAD_DIM + BF16_SUBLANES
ROW_TILE = 512


def _params(*semantics):
    return pltpu.CompilerParams(dimension_semantics=semantics,
                                vmem_limit_bytes=V7X_VMEM_LIMIT_BYTES)


def _resident(block_shape, index_map):
    return pl.BlockSpec(block_shape, index_map, pipeline_mode=pl.Buffered(1))


def _dot(a, b):
    return jnp.dot(a, b, preferred_element_type=F32)


def _split_bf16(x):
    hi = x.astype(BF16)
    lo = (x - hi.astype(F32)).astype(BF16)
    return hi, lo


def _dot3(a_hi, a_lo, b_hi, b_lo):
    return _dot(a_hi, b_hi) + _dot(a_hi, b_lo) + _dot(a_lo, b_hi)


def _layer_norm(x, g, b):
    mu = jnp.mean(x, axis=-1, keepdims=True)
    xc = x - mu
    var = jnp.mean(xc * xc, axis=-1, keepdims=True)
    return xc * lax.rsqrt(var + LN_EPS) * g + b


def _group_rms_norm(x, g):
    ms = jnp.mean(x * x, axis=-1, keepdims=True)
    return x * lax.rsqrt(ms + RMS_EPS) * g


A_COLS = 512
B_COLS = 768
C_COLS = 256
D_COLS = 512
D_IN_PROJ = A_COLS + B_COLS + C_COLS + D_COLS


def _rope_tables(seq_len):
    pos = jnp.arange(seq_len)
    inv_freq = ROPE_THETA ** (-jnp.arange(0, AXIS_DIM, 2, dtype=F32) / AXIS_DIM)
    ang_row = (pos // GRID_W).astype(F32)[:, None] * inv_freq[None, :]
    ang_col = (pos % GRID_W).astype(F32)[:, None] * inv_freq[None, :]
    ang = jnp.concatenate([ang_row, ang_row, ang_col, ang_col], axis=-1)
    half = AXIS_DIM // 2
    sign = np.tile(np.concatenate([-np.ones(half), np.ones(half)]), HEAD_DIM // AXIS_DIM)
    cos = jnp.tile(jnp.cos(ang), (1, LANES // HEAD_DIM))
    sin_signed = jnp.tile(jnp.sin(ang) * jnp.asarray(sign, F32), (1, LANES // HEAD_DIM))
    return cos, sin_signed


def _head_norm_rope(x, gain, cos, sin_signed, head_mean):
    width = x.shape[-1]
    sq_hi, sq_lo = _split_bf16(x * x)
    ms = _dot(sq_hi, head_mean) + _dot(sq_lo, head_mean)
    xn = x * lax.rsqrt(ms + RMS_EPS) * gain
    reps = width // LANES
    if reps > 1:
        cos = jnp.concatenate([cos] * reps, axis=-1)
        sin_signed = jnp.concatenate([sin_signed] * reps, axis=-1)
    half = AXIS_DIM // 2
    lane = lax.broadcasted_iota(jnp.int32, x.shape, 1)
    first_half = (lane % AXIS_DIM) < half
    partner = jnp.where(first_half, pltpu.roll(xn, width - half, axis=1),
                        pltpu.roll(xn, half, axis=1))
    return xn * cos + partner * sin_signed


def _store_attn_operands(q, k, v, qt_ref, k_ref, vt_ref):
    rows = q.shape[0]
    qt_ref[...] = (q * (HEAD_DIM ** -0.5 * LOG2_E)).T.astype(BF16)
    k_ref[...] = k.astype(BF16)
    vt = v.T.astype(BF16)
    pad = (lax.broadcasted_iota(jnp.int32, (VT_ROWS - HEAD_DIM, rows), 0) == 0).astype(BF16)
    vt_ref[...] = jnp.concatenate(
        [piece for kv in range(KV_HEADS)
         for piece in (vt[kv * HEAD_DIM:(kv + 1) * HEAD_DIM], pad)], axis=0)


def _in_proj_kernel(apply_ln, x_ref, g_ref, b_ref, w_ref, cos_ref, sin_ref, gq_ref, gk_ref,
                    hm_ref, *out_refs):
    x = x_ref[...]
    if apply_ln:
        h_ref, *out_refs = out_refs
        x = _layer_norm(x, g_ref[...], b_ref[...])
        h_ref[...] = x
    qta_ref, ka_ref, vta_ref, pb_ref, pc_ref, qtd_ref, kd_ref, vtd_ref = out_refs
    proj = _dot(x.astype(BF16), w_ref[...])
    a_q, a_k, a_v = (proj[:, :GROUP_WIDTH], proj[:, GROUP_WIDTH:GROUP_WIDTH + KV_WIDTH],
                     proj[:, GROUP_WIDTH + KV_WIDTH:A_COLS])
    _store_attn_operands(a_q, a_k, a_v, qta_ref, ka_ref, vta_ref)
    pb_ref[...] = proj[:, A_COLS:A_COLS + B_COLS]
    pc_ref[...] = proj[:, A_COLS + B_COLS:A_COLS + B_COLS + C_COLS]
    d0 = A_COLS + B_COLS + C_COLS
    d_q, d_k, d_v = (proj[:, d0:d0 + GROUP_WIDTH],
                     proj[:, d0 + GROUP_WIDTH:d0 + GROUP_WIDTH + KV_WIDTH],
                     proj[:, d0 + GROUP_WIDTH + KV_WIDTH:])
    cos, sin_signed, hm = cos_ref[...], sin_ref[...], hm_ref[...]
    d_q = _head_norm_rope(d_q, gq_ref[...], cos, sin_signed, hm)
    d_k = _head_norm_rope(d_k, gk_ref[...], cos, sin_signed, hm[:KV_WIDTH, :KV_WIDTH])
    _store_attn_operands(d_q, d_k, d_v, qtd_ref, kd_ref, vtd_ref)


def _in_proj(x, g, b, w, qn_g, kn_g, apply_ln, batch, seq_len):
    tm = min(ROW_TILE, seq_len)
    cos, sin_signed = _rope_tables(seq_len)
    gq = jnp.tile(qn_g, GROUP_WIDTH // HEAD_DIM).reshape(1, GROUP_WIDTH)
    gk = jnp.tile(kn_g, KV_WIDTH // HEAD_DIM).reshape(1, KV_WIDTH)
    head_mean = jnp.asarray(
        np.kron(np.eye(GROUP_WIDTH // HEAD_DIM), np.full((HEAD_DIM, HEAD_DIM), 1.0 / HEAD_DIM)),
        BF16)
    rows_spec = lambda width: pl.BlockSpec((None, tm, width), lambda bi, i: (bi, i, 0))
    cols_spec = lambda height: pl.BlockSpec((None, height, tm), lambda bi, i: (bi, 0, i))
    table_spec = pl.BlockSpec((tm, LANES), lambda bi, i: (i, 0))
    const = lambda shape: _resident(shape, lambda bi, i: (0, 0))
    rows_shape = lambda width, dt: jax.ShapeDtypeStruct((batch, seq_len, width), dt)
    cols_shape = lambda height: jax.ShapeDtypeStruct((batch, height, seq_len), BF16)
    attn_specs = [cols_spec(GROUP_WIDTH), rows_spec(KV_WIDTH), cols_spec(KV_HEADS * VT_ROWS)]
    attn_shapes = [cols_shape(GROUP_WIDTH), rows_shape(KV_WIDTH, BF16),
                   cols_shape(KV_HEADS * VT_ROWS)]
    out_specs = attn_specs + [rows_spec(B_COLS), rows_spec(C_COLS)] + attn_specs
    out_shape = attn_shapes + [rows_shape(B_COLS, F32), rows_shape(C_COLS, F32)] + attn_shapes
    if apply_ln:
        out_specs = [rows_spec(D_MODEL)] + out_specs
        out_shape = [rows_shape(D_MODEL, F32)] + out_shape
    return pl.pallas_call(
        functools.partial(_in_proj_kernel, apply_ln),
        grid=(batch, seq_len // tm),
        in_specs=[rows_spec(D_MODEL), const((1, D_MODEL)), const((1, D_MODEL)),
                  const((D_MODEL, D_IN_PROJ)), table_spec, table_spec,
                  const((1, GROUP_WIDTH)), const((1, KV_WIDTH)), const((GROUP_WIDTH, GROUP_WIDTH))],
        out_specs=out_specs,
        out_shape=out_shape,
        compiler_params=_params("parallel", "parallel"),
        name="in_proj_ln" if apply_ln else "in_proj",
    )(x, g, b, w, cos, sin_signed, gq, gk, head_mean)


WINDOW_TQ = 2048


def _window_bias_table():
    key = np.arange(3 * BLOCK)[:, None]
    query = np.arange(BLOCK)[None, :]
    dist = np.abs(BLOCK + query - key)
    per_head = [np.where(dist <= WINDOW, -slope * LOG2_E * dist, NEG_INF) for slope in ALIBI_SLOPES]
    groups = [np.concatenate(per_head[kv * GROUP:(kv + 1) * GROUP], axis=1) for kv in range(KV_HEADS)]
    return jnp.asarray(np.stack(groups), F32)


def _window_attn_kernel(qt_ref, km_ref, kp_ref, kn_ref, vm_ref, vp_ref, vn_ref, bias_ref,
                        sink_ref, g_ref, o_ref, kall_ref, vall_ref, s_ref):
    tq = qt_ref.shape[1]
    n_blk = tq // BLOCK
    i = pl.program_id(1)
    before_start = jnp.where(i == 0, NEG_INF, 0.0)
    after_end = jnp.where(i == pl.num_programs(1) - 1, NEG_INF, 0.0)

    kall_ref[0:BLOCK, :] = kp_ref[...]
    kall_ref[BLOCK:BLOCK + tq, :] = km_ref[...]
    kall_ref[BLOCK + tq:, :] = kn_ref[...]
    vall_ref[:, 0:BLOCK] = vp_ref[...]
    vall_ref[:, BLOCK:BLOCK + tq] = vm_ref[...]
    vall_ref[:, BLOCK + tq:] = vn_ref[...]
    zeros = jnp.zeros((HEAD_DIM, BLOCK), BF16)
    g = g_ref[...]

    def scores(jb, kv):
        cols = slice(jb * BLOCK, (jb + 1) * BLOCK)
        halves = []
        for h in range(kv * GROUP, (kv + 1) * GROUP):
            qh = qt_ref[h * HEAD_DIM:(h + 1) * HEAD_DIM, cols]
            halves.append(jnp.concatenate([qh, zeros] if kv == 0 else [zeros, qh], axis=0))
        kw = kall_ref[jb * BLOCK:(jb + 3) * BLOCK, :]
        s = _dot(kw, jnp.concatenate(halves, axis=1)) + bias_ref[kv]
        if jb == 0:
            s = jnp.concatenate([s[:BLOCK] + before_start, s[BLOCK:]], axis=0)
        if jb == n_blk - 1:
            s = jnp.concatenate([s[:2 * BLOCK], s[2 * BLOCK:] + after_end], axis=0)
        s_ref[kv] = s
        return jnp.max(s, axis=0, keepdims=True)

    def softmax_pv(jb, kv, s_max):
        sink = sink_ref[kv]
        m = jnp.maximum(s_max, sink)
        p = jnp.exp2(s_ref[kv] - m).astype(BF16)
        vt = vall_ref[kv * VT_ROWS:(kv + 1) * VT_ROWS, jb * BLOCK:(jb + 3) * BLOCK]
        return _dot(vt, p), jnp.exp2(sink - m)

    outs = []

    def finish(jb, kv, pv, sink_term):
        o = pv[:HEAD_DIM] / (pv[HEAD_DIM:HEAD_DIM + 1] + sink_term)
        outs.extend([o[:, :BLOCK], o[:, BLOCK:]])
        if kv == KV_HEADS - 1:
            out = jnp.concatenate(outs, axis=0).T
            o_ref[jb * BLOCK:(jb + 1) * BLOCK, :] = _group_rms_norm(out, g).astype(o_ref.dtype)
            outs.clear()

    units = [(jb, kv) for jb in range(n_blk) for kv in range(KV_HEADS)]
    s_max = scores(*units[0])
    pending = None
    for idx, unit in enumerate(units):
        nxt = scores(*units[idx + 1]) if idx + 1 < len(units) else None
        pv_parts = softmax_pv(*unit, s_max)
        if pending is not None:
            finish(*pending)
        pending = (*unit, *pv_parts)
        s_max = nxt
    finish(*pending)


def _window_attn(qt, k, vt, sink, g, batch, seq_len):
    tq = min(WINDOW_TQ, seq_len)
    per_tile = tq // BLOCK
    n_blocks = seq_len // BLOCK
    sink_rows = jnp.repeat(sink * LOG2_E, BLOCK).reshape(KV_HEADS, 1, GROUP * BLOCK)
    prev_idx = lambda i: jnp.maximum(i * per_tile - 1, 0)
    next_idx = lambda i: jnp.minimum((i + 1) * per_tile, n_blocks - 1)
    v_rows = KV_HEADS * VT_ROWS
    const3 = lambda shape: _resident(shape, lambda b, i: (0, 0, 0))
    return pl.pallas_call(
        _window_attn_kernel,
        grid=(batch, seq_len // tq),
        in_specs=[pl.BlockSpec((None, GROUP_WIDTH, tq), lambda b, i: (b, 0, i)),
                  pl.BlockSpec((None, tq, KV_WIDTH), lambda b, i: (b, i, 0)),
                  pl.BlockSpec((None, BLOCK, KV_WIDTH), lambda b, i: (b, prev_idx(i), 0)),
                  pl.BlockSpec((None, BLOCK, KV_WIDTH), lambda b, i: (b, next_idx(i), 0)),
                  pl.BlockSpec((None, v_rows, tq), lambda b, i: (b, 0, i)),
                  pl.BlockSpec((None, v_rows, BLOCK), lambda b, i: (b, 0, prev_idx(i))),
                  pl.BlockSpec((None, v_rows, BLOCK), lambda b, i: (b, 0, next_idx(i))),
                  const3((KV_HEADS, 3 * BLOCK, GROUP * BLOCK)),
                  const3((KV_HEADS, 1, GROUP * BLOCK)),
                  _resident((1, GROUP_WIDTH), lambda b, i: (0, 0))],
        out_specs=pl.BlockSpec((None, tq, GROUP_WIDTH), lambda b, i: (b, i, 0)),
        out_shape=jax.ShapeDtypeStruct((batch, seq_len, GROUP_WIDTH), BF16),
        scratch_shapes=[pltpu.VMEM((tq + 2 * BLOCK, KV_WIDTH), BF16),
                        pltpu.VMEM((v_rows, tq + 2 * BLOCK), BF16),
                        pltpu.VMEM((KV_HEADS, 3 * BLOCK, GROUP * BLOCK), F32)],
        compiler_params=_params("parallel", "parallel"),
        name="window_attn",
    )(qt, k, k, k, vt, vt, vt, _window_bias_table(), sink_rows, g)


def _short_conv_kernel(u_ref, gb_ref, gc_ref, up_ref, un_ref, cp_ref, cn_ref, w_ref, g_ref,
                       o_ref):
    tm = u_ref.shape[0]
    i = pl.program_id(1)
    last = pl.num_programs(1) - 1
    z = gc_ref[...] * u_ref[...]
    z_before = jnp.where(i > 0, cp_ref[SUBLANES - 1:SUBLANES, :] * up_ref[SUBLANES - 1:SUBLANES, :],
                         0.0)
    z_after = jnp.where(i < last, cn_ref[0:1, :] * un_ref[0:1, :], 0.0)
    row = lax.broadcasted_iota(jnp.int32, z.shape, 0)
    z_m1 = jnp.where(row == 0, z_before, pltpu.roll(z, 1, axis=0))
    z_p1 = jnp.where(row == tm - 1, z_after, pltpu.roll(z, tm - 1, axis=0))
    w = w_ref[...]
    y = gb_ref[...] * (w[0:1, :] * z_m1 + w[1:2, :] * z + w[2:3, :] * z_p1)
    o_ref[...] = _group_rms_norm(y, g_ref[...]).astype(o_ref.dtype)


def _short_conv(pb, conv_w, g, batch, seq_len):
    tm = min(ROW_TILE, seq_len)
    per_tile = tm // SUBLANES
    n_row_blocks = seq_len // SUBLANES
    u_col, gb_col, gc_col = 0, 1, 2

    def main(col):
        return pl.BlockSpec((None, tm, GROUP_WIDTH), lambda b, i: (b, i, col))

    def prev(col):
        return pl.BlockSpec((None, SUBLANES, GROUP_WIDTH),
                            lambda b, i: (b, jnp.maximum(i * per_tile - 1, 0), col))

    def nxt(col):
        return pl.BlockSpec((None, SUBLANES, GROUP_WIDTH),
                            lambda b, i: (b, jnp.minimum((i + 1) * per_tile, n_row_blocks - 1), col))

    return pl.pallas_call(
        _short_conv_kernel,
        grid=(batch, seq_len // tm),
        in_specs=[main(u_col), main(gb_col), main(gc_col),
                  prev(u_col), nxt(u_col), prev(gc_col), nxt(gc_col),
                  _resident((CONV_WIDTH, GROUP_WIDTH), lambda b, i: (0, 0)),
                  _resident((1, GROUP_WIDTH), lambda b, i: (0, 0))],
        out_specs=pl.BlockSpec((None, tm, GROUP_WIDTH), lambda b, i: (b, i, 0)),
        out_shape=jax.ShapeDtypeStruct((batch, seq_len, GROUP_WIDTH), BF16),
        compiler_params=_params("parallel", "parallel"),
        name="short_conv",
    )(pb, pb, pb, pb, pb, pb, pb, conv_w, g)


FOURIER_T2_PER_STEP = 8
FOURIER_K1_PER_STEP = SUBLANES


def _dft_cos_sin(n):
    idx = np.arange(n)
    ang = 2.0 * np.pi * ((idx[:, None] * idx[None, :]) % n) / n
    return np.cos(ang), np.sin(ang)


def _fourier_stage1_kernel(x_ref, wh_ref, wl_ref, tc_ref, ts_ref, o_ref):
    n1 = x_ref.shape[0]
    for j in range(x_ref.shape[1]):
        lanes = slice(j * GROUP_WIDTH, (j + 1) * GROUP_WIDTH)
        x_hi, x_lo = _split_bf16(x_ref[:, j, :])
        a = _dot3(wh_ref[...], wl_ref[...], x_hi, x_lo)
        a_re, a_im = a[:n1], a[n1:]
        tc, ts = tc_ref[:, lanes], ts_ref[:, lanes]
        o_ref[0, :, j, :] = a_re * tc + a_im * ts
        o_ref[1, :, j, :] = a_im * tc - a_re * ts


def _fourier_stage2_kernel(a_ref, w2h_ref, w2l_ref, w3h_ref, w3l_ref, g_ref, o_ref):
    n2 = a_ref.shape[2]
    for j in range(a_ref.shape[1]):
        a = jnp.concatenate([a_ref[0, j], a_ref[1, j]], axis=0)
        a_hi, a_lo = _split_bf16(a)
        y = _dot3(w2h_ref[...], w2l_ref[...], a_hi, a_lo)
        y2 = jnp.concatenate([y[:n2], y[n2:]], axis=1)
        y_hi, y_lo = _split_bf16(y2)
        z = _dot3(y_hi, y_lo, w3h_ref[...], w3l_ref[...])
        o_ref[:, j, :] = _group_rms_norm(z, g_ref[...]).astype(o_ref.dtype)


def _fourier(pc, g, batch, seq_len):
    n = math.isqrt(seq_len)
    assert n * n == seq_len and n % FOURIER_T2_PER_STEP == 0
    cols = FOURIER_T2_PER_STEP * GROUP_WIDTH

    cos_n, sin_n = _dft_cos_sin(n)
    w1_hi, w1_lo = _split_bf16(jnp.asarray(np.concatenate([cos_n, -sin_n], axis=0), F32))
    w2 = np.block([[cos_n, sin_n], [-sin_n, cos_n]])
    w2_hi, w2_lo = _split_bf16(jnp.asarray(w2, F32))
    cos_c, sin_c = _dft_cos_sin(HEAD_DIM)
    eye = np.eye(GROUP_WIDTH // HEAD_DIM)
    w3 = np.concatenate([np.kron(eye, cos_c), np.kron(eye, sin_c)], axis=0)
    w3_hi, w3_lo = _split_bf16(jnp.asarray(w3 / math.sqrt(seq_len * HEAD_DIM), F32))

    k1 = jnp.arange(n, dtype=jnp.int32)[:, None]
    t2 = jnp.arange(n, dtype=jnp.int32)[None, :]
    theta = (2.0 * math.pi / seq_len) * (k1 * t2).astype(F32)
    tw_cos = jnp.repeat(jnp.cos(theta), GROUP_WIDTH, axis=1)
    tw_sin = jnp.repeat(jnp.sin(theta), GROUP_WIDTH, axis=1)

    const = lambda shape: _resident(shape, lambda b, j: (0, 0))
    stage1 = pl.pallas_call(
        _fourier_stage1_kernel,
        grid=(batch, n // FOURIER_T2_PER_STEP),
        in_specs=[pl.BlockSpec((None, n, FOURIER_T2_PER_STEP, GROUP_WIDTH),
                               lambda b, j: (b, 0, j, 0)),
                  const((2 * n, n)), const((2 * n, n)),
                  pl.BlockSpec((n, cols), lambda b, j: (0, j)),
                  pl.BlockSpec((n, cols), lambda b, j: (0, j))],
        out_specs=pl.BlockSpec((None, 2, n, FOURIER_T2_PER_STEP, GROUP_WIDTH),
                               lambda b, j: (b, 0, 0, j, 0)),
        out_shape=jax.ShapeDtypeStruct((batch, 2, n, n, GROUP_WIDTH), F32),
        compiler_params=_params("parallel", "parallel"),
        name="fourier_stage1",
    )(pc.reshape(batch, n, n, GROUP_WIDTH), w1_hi, w1_lo, tw_cos, tw_sin)

    out = pl.pallas_call(
        _fourier_stage2_kernel,
        grid=(batch, n // FOURIER_K1_PER_STEP),
        in_specs=[pl.BlockSpec((None, 2, FOURIER_K1_PER_STEP, n, GROUP_WIDTH),
                               lambda b, j: (b, 0, j, 0, 0)),
                  const((2 * n, 2 * n)), const((2 * n, 2 * n)),
                  const((2 * GROUP_WIDTH, GROUP_WIDTH)), const((2 * GROUP_WIDTH, GROUP_WIDTH)),
                  const((1, GROUP_WIDTH))],
        out_specs=pl.BlockSpec((None, n, FOURIER_K1_PER_STEP, GROUP_WIDTH),
                               lambda b, j: (b, 0, j, 0)),
        out_shape=jax.ShapeDtypeStruct((batch, n, n, GROUP_WIDTH), F32),
        compiler_params=_params("parallel", "parallel"),
        name="fourier_stage2",
    )(stage1, w2_hi, w2_lo, w3_hi, w3_lo, g)
    return out.reshape(batch, seq_len, GROUP_WIDTH)


DENSE_TQ = 512
DENSE_TK = 1024
DENSE_UNROLL = 2


def _dense_attn_kernel(tk, unroll, qt_ref, k_ref, vt_ref, g_ref, o_ref, qpad_ref, s_ref, m_ref,
                       acc_ref):
    tq = qt_ref.shape[1]
    n_chunks = k_ref.shape[0] // tk

    m_ref[...] = jnp.full(m_ref.shape, -jnp.inf, F32)
    acc_ref[...] = jnp.zeros(acc_ref.shape, F32)
    zeros = jnp.zeros((HEAD_DIM, tq), BF16)
    for h in range(Q_HEADS):
        qh = qt_ref[h * HEAD_DIM:(h + 1) * HEAD_DIM, :]
        qpad_ref[h] = jnp.concatenate([qh, zeros] if h < GROUP else [zeros, qh], axis=0)

    def scores(j, h):
        start = pl.multiple_of(j * tk, tk)
        s = _dot(k_ref[pl.ds(start, tk), :], qpad_ref[h])
        s_ref[h % 2] = s
        return jnp.max(s, axis=0, keepdims=True)

    def softmax_pv(j, h, s_max):
        start = pl.multiple_of(j * tk, tk)
        m_old = m_ref[h:h + 1, :]
        m_new = jnp.maximum(m_old, s_max)
        p = jnp.exp2(s_ref[h % 2] - m_new).astype(BF16)
        alpha = jnp.exp2(m_old - m_new)
        kv = h // GROUP
        vt = vt_ref[kv * VT_ROWS:(kv + 1) * VT_ROWS, pl.ds(start, tk)]
        rows = slice(h * VT_ROWS, (h + 1) * VT_ROWS)
        acc_ref[rows, :] = alpha * acc_ref[rows, :] + _dot(vt, p)
        m_ref[h:h + 1, :] = m_new

    def chunk(j, s_max):
        for h in range(Q_HEADS):
            if h + 1 < Q_HEADS:
                nxt = scores(j, h + 1)
            else:
                nxt = scores(jnp.minimum(j + 1, n_chunks - 1), 0)
            softmax_pv(j, h, s_max)
            s_max = nxt
        return s_max

    lax.fori_loop(0, n_chunks, chunk, scores(0, 0), unroll=unroll)

    parts = []
    for h in range(Q_HEADS):
        row_sum = acc_ref[h * VT_ROWS + HEAD_DIM:h * VT_ROWS + HEAD_DIM + 1, :]
        parts.append(acc_ref[h * VT_ROWS:h * VT_ROWS + HEAD_DIM, :] / row_sum)
    out = jnp.concatenate(parts, axis=0).T
    o_ref[...] = _group_rms_norm(out, g_ref[...]).astype(o_ref.dtype)


def _dense_attn(qt, k, vt, g, batch, seq_len):
    tq = min(DENSE_TQ, seq_len)
    tk = min(DENSE_TK, seq_len)
    return pl.pallas_call(
        functools.partial(_dense_attn_kernel, tk, DENSE_UNROLL),
        grid=(batch, seq_len // tq),
        in_specs=[pl.BlockSpec((None, GROUP_WIDTH, tq), lambda b, i: (b, 0, i)),
                  pl.BlockSpec((None, seq_len, KV_WIDTH), lambda b, i: (b, 0, 0)),
                  pl.BlockSpec((None, KV_HEADS * VT_ROWS, seq_len), lambda b, i: (b, 0, 0)),
                  _resident((1, GROUP_WIDTH), lambda b, i: (0, 0))],
        out_specs=pl.BlockSpec((None, tq, GROUP_WIDTH), lambda b, i: (b, i, 0)),
        out_shape=jax.ShapeDtypeStruct((batch, seq_len, GROUP_WIDTH), BF16),
        scratch_shapes=[pltpu.VMEM((Q_HEADS, KV_WIDTH, tq), BF16),
                        pltpu.VMEM((2, tk, tq), F32),
                        pltpu.VMEM((Q_HEADS, tq), F32),
                        pltpu.VMEM((Q_HEADS * VT_ROWS, tq), F32)],
        compiler_params=_params("parallel", "parallel"),
        name="dense_attn",
    )(qt, k, vt, g)


FF_CHUNK = 1024


def _out_mlp_kernel(ma_ref, mb_ref, mc_ref, md_ref, h_ref, wo_ref, g1_ref, b1_ref,
                    w1_ref, w2_ref, g2_ref, b2_ref, o_ref):
    mix = jnp.concatenate([r[...].astype(BF16) for r in (ma_ref, mb_ref, mc_ref, md_ref)], axis=-1)
    h1 = _layer_norm(DN_ALPHA * h_ref[...] + _dot(mix, wo_ref[...]), g1_ref[...], b1_ref[...])
    h1_bf = h1.astype(BF16)
    ffn = jnp.zeros(h1.shape, F32)
    for c in range(D_FF // FF_CHUNK):
        cols = slice(c * FF_CHUNK, (c + 1) * FF_CHUNK)
        u = jnp.maximum(_dot(h1_bf, w1_ref[:, cols]), 0.0)
        ffn = ffn + _dot((u * u).astype(BF16), w2_ref[cols, :])
    o_ref[...] = _layer_norm(DN_ALPHA * h1 + ffn, g2_ref[...], b2_ref[...])


def _out_mlp(mixes, h, w_out, g1, b1, w1, w2, g2, b2):
    rows = h.shape[0]
    tm = min(ROW_TILE, rows)
    row_spec = lambda width: pl.BlockSpec((tm, width), lambda i: (i, 0))
    const = lambda shape: _resident(shape, lambda i: (0, 0))
    vec = const((1, D_MODEL))
    return pl.pallas_call(
        _out_mlp_kernel,
        grid=(rows // tm,),
        in_specs=[row_spec(GROUP_WIDTH)] * 4 + [row_spec(D_MODEL), const((D_MODEL, D_MODEL)),
                                                 vec, vec, const((D_MODEL, D_FF)),
                                                 const((D_FF, D_MODEL)), vec, vec],
        out_specs=row_spec(D_MODEL),
        out_shape=jax.ShapeDtypeStruct((rows, D_MODEL), F32),
        compiler_params=_params("parallel"),
        name="out_mlp",
    )(*mixes, h, w_out, g1, b1, w1, w2, g2, b2)


def kernel(x, ln_in_g, ln_in_b, w_in, conv_w, sink, qn_g, kn_g, grp_g, w_out, ln1_g, ln1_b,
           w1, w2, ln2_g, ln2_b):
    batch, seq_len, _ = x.shape
    rows = batch * seq_len
    depth = w_in.shape[0]
    row_vec = lambda v: v.reshape(1, -1)

    h = x
    for l in range(depth):
        first = l == 0
        outs = _in_proj(h, row_vec(ln_in_g), row_vec(ln_in_b), w_in[l].astype(BF16), qn_g[l],
                        kn_g[l], first, batch, seq_len)
        if first:
            h, *outs = outs
        qta, ka, vta, pb, pc, qtd, kd, vtd = outs
        g = [row_vec(grp_g[l, i * GROUP_WIDTH:(i + 1) * GROUP_WIDTH]) for i in range(4)]
        mix_a = _window_attn(qta, ka, vta, sink[l], g[0], batch, seq_len)
        mix_b = _short_conv(pb, conv_w[l], g[1], batch, seq_len)
        mix_c = _fourier(pc, g[2], batch, seq_len)
        mix_d = _dense_attn(qtd, kd, vtd, g[3], batch, seq_len)
        mixes = [m.reshape(rows, GROUP_WIDTH) for m in (mix_a, mix_b, mix_c, mix_d)]
        h = _out_mlp(mixes, h.reshape(rows, D_MODEL), w_out[l].astype(BF16), row_vec(ln1_g[l]),
                     row_vec(ln1_b[l]), w1[l].astype(BF16), w2[l].astype(BF16), row_vec(ln2_g[l]),
                     row_vec(ln2_b[l])).reshape(batch, seq_len, D_MODEL)
    return h
```

```python
import functools
import math

import numpy as np
import jax
import jax.numpy as jnp
from jax import lax
from jax.experimental import pallas as pl
from jax.experimental.pallas import tpu as pltpu

F32 = jnp.float32
BF16 = jnp.bfloat16

D_MODEL = 1024
HEAD_DIM = 64
GROUP_WIDTH = 256
Q_HEADS = 4
KV_HEADS = 2
GROUP = Q_HEADS // KV_HEADS
KV_WIDTH = KV_HEADS * HEAD_DIM
CONV_WIDTH = 3
WINDOW = 128
BLOCK = 128
GRID_W = 64
AXIS_DIM = HEAD_DIM // 2
ROPE_THETA = 10000.0
D_FF = 4 * D_MODEL
LN_EPS = 1e-5
RMS_EPS = 1e-6
DEPTH = 2
DN_ALPHA = (2 * DEPTH) ** 0.25
NEG_INF = -1e30
LOG2_E = math.log2(math.e)
ALIBI_SLOPES = tuple(2.0 ** (-8.0 * (h + 1) / Q_HEADS) for h in range(Q_HEADS))

V7X_VMEM_LIMIT_BYTES = 56 * 1024 * 1024
LANES = 128
SUBLANES = 8
BF16_SUBLANES = 16
VT_ROWS = HEAD_DIM + BF16_SUBLANES
ROW_TILE = 512


def _params(*semantics):
    return pltpu.CompilerParams(dimension_semantics=semantics,
                                vmem_limit_bytes=V7X_VMEM_LIMIT_BYTES)


def _resident(block_shape, index_map):
    return pl.BlockSpec(block_shape, index_map, pipeline_mode=pl.Buffered(1))


def _dot(a, b):
    return jnp.dot(a, b, preferred_element_type=F32)


def _split_bf16(x):
    hi = x.astype(BF16)
    lo = (x - hi.astype(F32)).astype(BF16)
    return hi, lo


def _dot3(a_hi, a_lo, b_hi, b_lo):
    return _dot(a_hi, b_hi) + _dot(a_hi, b_lo) + _dot(a_lo, b_hi)


def _layer_norm(x, g, b):
    mu = jnp.mean(x, axis=-1, keepdims=True)
    xc = x - mu
    var = jnp.mean(xc * xc, axis=-1, keepdims=True)
    return xc * lax.rsqrt(var + LN_EPS) * g + b


def _group_rms_norm(x, g):
    ms = jnp.mean(x * x, axis=-1, keepdims=True)
    return x * lax.rsqrt(ms + RMS_EPS) * g


A_COLS = 512
B_COLS = 768
C_COLS = 256
D_COLS = 512
D_IN_PROJ = A_COLS + B_COLS + C_COLS + D_COLS


def _rope_tables(seq_len):
    pos = jnp.arange(seq_len)
    inv_freq = ROPE_THETA ** (-jnp.arange(0, AXIS_DIM, 2, dtype=F32) / AXIS_DIM)
    ang_row = (pos // GRID_W).astype(F32)[:, None] * inv_freq[None, :]
    ang_col = (pos % GRID_W).astype(F32)[:, None] * inv_freq[None, :]
    ang = jnp.concatenate([ang_row, ang_row, ang_col, ang_col], axis=-1)
    half = AXIS_DIM // 2
    sign = np.tile(np.concatenate([-np.ones(half), np.ones(half)]), HEAD_DIM // AXIS_DIM)
    cos = jnp.tile(jnp.cos(ang), (1, LANES // HEAD_DIM))
    sin_signed = jnp.tile(jnp.sin(ang) * jnp.asarray(sign, F32), (1, LANES // HEAD_DIM))
    return cos, sin_signed


def _head_norm_rope(x, gain, cos, sin_signed, head_mean):
    width = x.shape[-1]
    sq_hi, sq_lo = _split_bf16(x * x)
    ms = _dot(sq_hi, head_mean) + _dot(sq_lo, head_mean)
    xn = x * lax.rsqrt(ms + RMS_EPS) * gain
    reps = width // LANES
    if reps > 1:
        cos = jnp.concatenate([cos] * reps, axis=-1)
        sin_signed = jnp.concatenate([sin_signed] * reps, axis=-1)
    half = AXIS_DIM // 2
    lane = lax.broadcasted_iota(jnp.int32, x.shape, 1)
    first_half = (lane % AXIS_DIM) < half
    partner = jnp.where(first_half, pltpu.roll(xn, width - half, axis=1),
                        pltpu.roll(xn, half, axis=1))
    return xn * cos + partner * sin_signed


def _store_attn_operands(q, k, v, qt_ref, k_ref, vt_ref):
    rows = q.shape[0]
    qt_ref[...] = (q * (HEAD_DIM ** -0.5 * LOG2_E)).T.astype(BF16)
    k_ref[...] = k.astype(BF16)
    vt = v.T.astype(BF16)
    pad = (lax.broadcasted_iota(jnp.int32, (VT_ROWS - HEAD_DIM, rows), 0) == 0).astype(BF16)
    vt_ref[...] = jnp.concatenate(
        [piece for kv in range(KV_HEADS)
         for piece in (vt[kv * HEAD_DIM:(kv + 1) * HEAD_DIM], pad)], axis=0)


def _in_proj_kernel(apply_ln, x_ref, g_ref, b_ref, w_ref, cos_ref, sin_ref, gq_ref, gk_ref,
                    hm_ref, *out_refs):
    x = x_ref[...]
    if apply_ln:
        h_ref, *out_refs = out_refs
        x = _layer_norm(x, g_ref[...], b_ref[...])
        h_ref[...] = x
    qta_ref, ka_ref, vta_ref, pb_ref, pc_ref, qtd_ref, kd_ref, vtd_ref = out_refs
    proj = _dot(x.astype(BF16), w_ref[...])
    a_q, a_k, a_v = (proj[:, :GROUP_WIDTH], proj[:, GROUP_WIDTH:GROUP_WIDTH + KV_WIDTH],
                     proj[:, GROUP_WIDTH + KV_WIDTH:A_COLS])
    _store_attn_operands(a_q, a_k, a_v, qta_ref, ka_ref, vta_ref)
    pb_ref[...] = proj[:, A_COLS:A_COLS + B_COLS]
    pc_ref[...] = proj[:, A_COLS + B_COLS:A_COLS + B_COLS + C_COLS]
    d0 = A_COLS + B_COLS + C_COLS
    d_q, d_k, d_v = (proj[:, d0:d0 + GROUP_WIDTH],
                     proj[:, d0 + GROUP_WIDTH:d0 + GROUP_WIDTH + KV_WIDTH],
                     proj[:, d0 + GROUP_WIDTH + KV_WIDTH:])
    cos, sin_signed, hm = cos_ref[...], sin_ref[...], hm_ref[...]
    d_q = _head_norm_rope(d_q, gq_ref[...], cos, sin_signed, hm)
    d_k = _head_norm_rope(d_k, gk_ref[...], cos, sin_signed, hm[:KV_WIDTH, :KV_WIDTH])
    _store_attn_operands(d_q, d_k, d_v, qtd_ref, kd_ref, vtd_ref)


def _in_proj(x, g, b, w, qn_g, kn_g, apply_ln, batch, seq_len):
    tm = min(ROW_TILE, seq_len)
    cos, sin_signed = _rope_tables(seq_len)
    gq = jnp.tile(qn_g, GROUP_WIDTH // HEAD_DIM).reshape(1, GROUP_WIDTH)
    gk = jnp.tile(kn_g, KV_WIDTH // HEAD_DIM).reshape(1, KV_WIDTH)
    head_mean = jnp.asarray(
        np.kron(np.eye(GROUP_WIDTH // HEAD_DIM), np.full((HEAD_DIM, HEAD_DIM), 1.0 / HEAD_DIM)),
        BF16)
    rows_spec = lambda width: pl.BlockSpec((None, tm, width), lambda bi, i: (bi, i, 0))
    cols_spec = lambda height: pl.BlockSpec((None, height, tm), lambda bi, i: (bi, 0, i))
    table_spec = pl.BlockSpec((tm, LANES), lambda bi, i: (i, 0))
    const = lambda shape: _resident(shape, lambda bi, i: (0, 0))
    rows_shape = lambda width, dt: jax.ShapeDtypeStruct((batch, seq_len, width), dt)
    cols_shape = lambda height: jax.ShapeDtypeStruct((batch, height, seq_len), BF16)
    attn_specs = [cols_spec(GROUP_WIDTH), rows_spec(KV_WIDTH), cols_spec(KV_HEADS * VT_ROWS)]
    attn_shapes = [cols_shape(GROUP_WIDTH), rows_shape(KV_WIDTH, BF16),
                   cols_shape(KV_HEADS * VT_ROWS)]
    out_specs = attn_specs + [rows_spec(B_COLS), rows_spec(C_COLS)] + attn_specs
    out_shape = attn_shapes + [rows_shape(B_COLS, F32), rows_shape(C_COLS, F32)] + attn_shapes
    if apply_ln:
        out_specs = [rows_spec(D_MODEL)] + out_specs
        out_shape = [rows_shape(D_MODEL, F32)] + out_shape
    return pl.pallas_call(
        functools.partial(_in_proj_kernel, apply_ln),
        grid=(batch, seq_len // tm),
        in_specs=[rows_spec(D_MODEL), const((1, D_MODEL)), const((1, D_MODEL)),
                  const((D_MODEL, D_IN_PROJ)), table_spec, table_spec,
                  const((1, GROUP_WIDTH)), const((1, KV_WIDTH)), const((GROUP_WIDTH, GROUP_WIDTH))],
        out_specs=out_specs,
        out_shape=out_shape,
        compiler_params=_params("parallel", "parallel"),
        name="in_proj_ln" if apply_ln else "in_proj",
    )(x, g, b, w, cos, sin_signed, gq, gk, head_mean)


WINDOW_TQ = 2048


def _window_bias_table():
    key = np.arange(3 * BLOCK)[:, None]
    query = np.arange(BLOCK)[None, :]
    dist = np.abs(BLOCK + query - key)
    per_head = [np.where(dist <= WINDOW, -slope * LOG2_E * dist, NEG_INF) for slope in ALIBI_SLOPES]
    groups = [np.concatenate(per_head[kv * GROUP:(kv + 1) * GROUP], axis=1) for kv in range(KV_HEADS)]
    return jnp.asarray(np.stack(groups), F32)


def _window_attn_kernel(qt_ref, km_ref, kp_ref, kn_ref, vm_ref, vp_ref, vn_ref, bias_ref,
                        sink_ref, g_ref, o_ref, kall_ref, vall_ref, s_ref):
    tq = qt_ref.shape[1]
    n_blk = tq // BLOCK
    i = pl.program_id(1)
    before_start = jnp.where(i == 0, NEG_INF, 0.0)
    after_end = jnp.where(i == pl.num_programs(1) - 1, NEG_INF, 0.0)

    kall_ref[0:BLOCK, :] = kp_ref[...]
    kall_ref[BLOCK:BLOCK + tq, :] = km_ref[...]
    kall_ref[BLOCK + tq:, :] = kn_ref[...]
    vall_ref[:, 0:BLOCK] = vp_ref[...]
    vall_ref[:, BLOCK:BLOCK + tq] = vm_ref[...]
    vall_ref[:, BLOCK + tq:] = vn_ref[...]
    zeros = jnp.zeros((HEAD_DIM, BLOCK), BF16)
    g = g_ref[...]

    def scores(jb, kv):
        cols = slice(jb * BLOCK, (jb + 1) * BLOCK)
        halves = []
        for h in range(kv * GROUP, (kv + 1) * GROUP):
            qh = qt_ref[h * HEAD_DIM:(h + 1) * HEAD_DIM, cols]
            halves.append(jnp.concatenate([qh, zeros] if kv == 0 else [zeros, qh], axis=0))
        kw = kall_ref[jb * BLOCK:(jb + 3) * BLOCK, :]
        s = _dot(kw, jnp.concatenate(halves, axis=1)) + bias_ref[kv]
        if jb == 0:
            s = jnp.concatenate([s[:BLOCK] + before_start, s[BLOCK:]], axis=0)
        if jb == n_blk - 1:
            s = jnp.concatenate([s[:2 * BLOCK], s[2 * BLOCK:] + after_end], axis=0)
        s_ref[kv] = s
        return jnp.max(s, axis=0, keepdims=True)

    def softmax_pv(jb, kv, s_max):
        sink = sink_ref[kv]
        m = jnp.maximum(s_max, sink)
        p = jnp.exp2(s_ref[kv] - m).astype(BF16)
        vt = vall_ref[kv * VT_ROWS:(kv + 1) * VT_ROWS, jb * BLOCK:(jb + 3) * BLOCK]
        return _dot(vt, p), jnp.exp2(sink - m)

    outs = []

    def finish(jb, kv, pv, sink_term):
        o = pv[:HEAD_DIM] / (pv[HEAD_DIM:HEAD_DIM + 1] + sink_term)
        outs.extend([o[:, :BLOCK], o[:, BLOCK:]])
        if kv == KV_HEADS - 1:
            out = jnp.concatenate(outs, axis=0).T
            o_ref[jb * BLOCK:(jb + 1) * BLOCK, :] = _group_rms_norm(out, g).astype(o_ref.dtype)
            outs.clear()

    units = [(jb, kv) for jb in range(n_blk) for kv in range(KV_HEADS)]
    s_max = scores(*units[0])
    pending = None
    for idx, unit in enumerate(units):
        nxt = scores(*units[idx + 1]) if idx + 1 < len(units) else None
        pv_parts = softmax_pv(*unit, s_max)
        if pending is not None:
            finish(*pending)
        pending = (*unit, *pv_parts)
        s_max = nxt
    finish(*pending)


def _window_attn(qt, k, vt, sink, g, batch, seq_len):
    tq = min(WINDOW_TQ, seq_len)
    per_tile = tq // BLOCK
    n_blocks = seq_len // BLOCK
    sink_rows = jnp.repeat(sink * LOG2_E, BLOCK).reshape(KV_HEADS, 1, GROUP * BLOCK)
    prev_idx = lambda i: jnp.maximum(i * per_tile - 1, 0)
    next_idx = lambda i: jnp.minimum((i + 1) * per_tile, n_blocks - 1)
    v_rows = KV_HEADS * VT_ROWS
    const3 = lambda shape: _resident(shape, lambda b, i: (0, 0, 0))
    return pl.pallas_call(
        _window_attn_kernel,
        grid=(batch, seq_len // tq),
        in_specs=[pl.BlockSpec((None, GROUP_WIDTH, tq), lambda b, i: (b, 0, i)),
                  pl.BlockSpec((None, tq, KV_WIDTH), lambda b, i: (b, i, 0)),
                  pl.BlockSpec((None, BLOCK, KV_WIDTH), lambda b, i: (b, prev_idx(i), 0)),
                  pl.BlockSpec((None, BLOCK, KV_WIDTH), lambda b, i: (b, next_idx(i), 0)),
                  pl.BlockSpec((None, v_rows, tq), lambda b, i: (b, 0, i)),
                  pl.BlockSpec((None, v_rows, BLOCK), lambda b, i: (b, 0, prev_idx(i))),
                  pl.BlockSpec((None, v_rows, BLOCK), lambda b, i: (b, 0, next_idx(i))),
                  const3((KV_HEADS, 3 * BLOCK, GROUP * BLOCK)),
                  const3((KV_HEADS, 1, GROUP * BLOCK)),
                  _resident((1, GROUP_WIDTH), lambda b, i: (0, 0))],
        out_specs=pl.BlockSpec((None, tq, GROUP_WIDTH), lambda b, i: (b, i, 0)),
        out_shape=jax.ShapeDtypeStruct((batch, seq_len, GROUP_WIDTH), BF16),
        scratch_shapes=[pltpu.VMEM((tq + 2 * BLOCK, KV_WIDTH), BF16),
                        pltpu.VMEM((v_rows, tq + 2 * BLOCK), BF16),
                        pltpu.VMEM((KV_HEADS, 3 * BLOCK, GROUP * BLOCK), F32)],
        compiler_params=_params("parallel", "parallel"),
        name="window_attn",
    )(qt, k, k, k, vt, vt, vt, _window_bias_table(), sink_rows, g)


def _short_conv_kernel(u_ref, gb_ref, gc_ref, up_ref, un_ref, cp_ref, cn_ref, w_ref, g_ref,
                       o_ref):
    tm = u_ref.shape[0]
    i = pl.program_id(1)
    last = pl.num_programs(1) - 1
    z = gc_ref[...] * u_ref[...]
    z_before = jnp.where(i > 0, cp_ref[SUBLANES - 1:SUBLANES, :] * up_ref[SUBLANES - 1:SUBLANES, :],
                         0.0)
    z_after = jnp.where(i < last, cn_ref[0:1, :] * un_ref[0:1, :], 0.0)
    row = lax.broadcasted_iota(jnp.int32, z.shape, 0)
    z_m1 = jnp.where(row == 0, z_before, pltpu.roll(z, 1, axis=0))
    z_p1 = jnp.where(row == tm - 1, z_after, pltpu.roll(z, tm - 1, axis=0))
    w = w_ref[...]
    y = gb_ref[...] * (w[0:1, :] * z_m1 + w[1:2, :] * z + w[2:3, :] * z_p1)
    o_ref[...] = _group_rms_norm(y, g_ref[...]).astype(o_ref.dtype)


def _short_conv(pb, conv_w, g, batch, seq_len):
    tm = min(ROW_TILE, seq_len)
    per_tile = tm // SUBLANES
    n_row_blocks = seq_len // SUBLANES
    u_col, gb_col, gc_col = 0, 1, 2

    def main(col):
        return pl.BlockSpec((None, tm, GROUP_WIDTH), lambda b, i: (b, i, col))

    def prev(col):
        return pl.BlockSpec((None, SUBLANES, GROUP_WIDTH),
                            lambda b, i: (b, jnp.maximum(i * per_tile - 1, 0), col))

    def nxt(col):
        return pl.BlockSpec((None, SUBLANES, GROUP_WIDTH),
                            lambda b, i: (b, jnp.minimum((i + 1) * per_tile, n_row_blocks - 1), col))

    return pl.pallas_call(
        _short_conv_kernel,
        grid=(batch, seq_len // tm),
        in_specs=[main(u_col), main(gb_col), main(gc_col),
                  prev(u_col), nxt(u_col), prev(gc_col), nxt(gc_col),
                  _resident((CONV_WIDTH, GROUP_WIDTH), lambda b, i: (0, 0)),
                  _resident((1, GROUP_WIDTH), lambda b, i: (0, 0))],
        out_specs=pl.BlockSpec((None, tm, GROUP_WIDTH), lambda b, i: (b, i, 0)),
        out_shape=jax.ShapeDtypeStruct((batch, seq_len, GROUP_WIDTH), BF16),
        compiler_params=_params("parallel", "parallel"),
        name="short_conv",
    )(pb, pb, pb, pb, pb, pb, pb, conv_w, g)


FOURIER_T2_PER_STEP = 8
FOURIER_K1_PER_STEP = SUBLANES


def _dft_cos_sin(n):
    idx = np.arange(n)
    ang = 2.0 * np.pi * ((idx[:, None] * idx[None, :]) % n) / n
    return np.cos(ang), np.sin(ang)


def _fourier_stage1_kernel(x_ref, wh_ref, wl_ref, tc_ref, ts_ref, o_ref):
    n1 = x_ref.shape[0]
    x_hi, x_lo = _split_bf16(x_ref[...])
    a = _dot3(wh_ref[...], wl_ref[...], x_hi, x_lo)
    a_re, a_im = a[:n1], a[n1:]
    tc, ts = tc_ref[...], ts_ref[...]
    o_ref[0] = a_re * tc + a_im * ts
    o_ref[1] = a_im * tc - a_re * ts


def _fourier_stage2_kernel(a_ref, w2h_ref, w2l_ref, w3h_ref, w3l_ref, g_ref, o_ref):
    n2 = a_ref.shape[2]
    for j in range(a_ref.shape[1]):
        a = jnp.concatenate([a_ref[0, j], a_ref[1, j]], axis=0)
        a_hi, a_lo = _split_bf16(a)
        y = _dot3(w2h_ref[...], w2l_ref[...], a_hi, a_lo)
        y2 = jnp.concatenate([y[:n2], y[n2:]], axis=1)
        y_hi, y_lo = _split_bf16(y2)
        z = _dot3(y_hi, y_lo, w3h_ref[...], w3l_ref[...])
        o_ref[:, j, :] = _group_rms_norm(z, g_ref[...]).astype(o_ref.dtype)


def _fourier(pc, g, batch, seq_len):
    n = math.isqrt(seq_len)
    assert n * n == seq_len and n % FOURIER_T2_PER_STEP == 0
    cols = FOURIER_T2_PER_STEP * GROUP_WIDTH

    cos_n, sin_n = _dft_cos_sin(n)
    w1_hi, w1_lo = _split_bf16(jnp.asarray(np.concatenate([cos_n, -sin_n], axis=0), F32))
    w2 = np.block([[cos_n, sin_n], [-sin_n, cos_n]])
    w2_hi, w2_lo = _split_bf16(jnp.asarray(w2, F32))
    cos_c, sin_c = _dft_cos_sin(HEAD_DIM)
    eye = np.eye(GROUP_WIDTH // HEAD_DIM)
    w3 = np.concatenate([np.kron(eye, cos_c), np.kron(eye, sin_c)], axis=0)
    w3_hi, w3_lo = _split_bf16(jnp.asarray(w3 / math.sqrt(seq_len * HEAD_DIM), F32))

    k1 = jnp.arange(n, dtype=jnp.int32)[:, None]
    t2 = jnp.arange(n, dtype=jnp.int32)[None, :]
    theta = (2.0 * math.pi / seq_len) * (k1 * t2).astype(F32)
    tw_cos = jnp.repeat(jnp.cos(theta), GROUP_WIDTH, axis=1)
    tw_sin = jnp.repeat(jnp.sin(theta), GROUP_WIDTH, axis=1)

    const = lambda shape: _resident(shape, lambda b, j: (0, 0))
    stage1 = pl.pallas_call(
        _fourier_stage1_kernel,
        grid=(batch, n // FOURIER_T2_PER_STEP),
        in_specs=[pl.BlockSpec((None, n, cols), lambda b, j: (b, 0, j)),
                  const((2 * n, n)), const((2 * n, n)),
                  pl.BlockSpec((n, cols), lambda b, j: (0, j)),
                  pl.BlockSpec((n, cols), lambda b, j: (0, j))],
        out_specs=pl.BlockSpec((None, 2, n, cols), lambda b, j: (b, 0, 0, j)),
        out_shape=jax.ShapeDtypeStruct((batch, 2, n, n * GROUP_WIDTH), F32),
        compiler_params=_params("parallel", "parallel"),
        name="fourier_stage1",
    )(pc.reshape(batch, n, n * GROUP_WIDTH), w1_hi, w1_lo, tw_cos, tw_sin)

    out = pl.pallas_call(
        _fourier_stage2_kernel,
        grid=(batch, n // FOURIER_K1_PER_STEP),
        in_specs=[pl.BlockSpec((None, 2, FOURIER_K1_PER_STEP, n, GROUP_WIDTH),
                               lambda b, j: (b, 0, j, 0, 0)),
                  const((2 * n, 2 * n)), const((2 * n, 2 * n)),
                  const((2 * GROUP_WIDTH, GROUP_WIDTH)), const((2 * GROUP_WIDTH, GROUP_WIDTH)),
                  const((1, GROUP_WIDTH))],
        out_specs=pl.BlockSpec((None, n, FOURIER_K1_PER_STEP, GROUP_WIDTH),
                               lambda b, j: (b, 0, j, 0)),
        out_shape=jax.ShapeDtypeStruct((batch, n, n, GROUP_WIDTH), F32),
        compiler_params=_params("parallel", "parallel"),
        name="fourier_stage2",
    )(stage1.reshape(batch, 2, n, n, GROUP_WIDTH), w2_hi, w2_lo, w3_hi, w3_lo, g)
    return out.reshape(batch, seq_len, GROUP_WIDTH)


DENSE_TQ = 512
DENSE_TK = 1024
DENSE_UNROLL = 2


def _dense_attn_kernel(tk, unroll, qt_ref, k_ref, vt_ref, g_ref, o_ref, qpad_ref, s_ref, m_ref,
                       acc_ref):
    tq = qt_ref.shape[1]
    n_chunks = k_ref.shape[0] // tk

    m_ref[...] = jnp.full(m_ref.shape, -jnp.inf, F32)
    acc_ref[...] = jnp.zeros(acc_ref.shape, F32)
    zeros = jnp.zeros((HEAD_DIM, tq), BF16)
    for h in range(Q_HEADS):
        qh = qt_ref[h * HEAD_DIM:(h + 1) * HEAD_DIM, :]
        qpad_ref[h] = jnp.concatenate([qh, zeros] if h < GROUP else [zeros, qh], axis=0)

    def scores(j, h):
        start = pl.multiple_of(j * tk, tk)
        s = _dot(k_ref[pl.ds(start, tk), :], qpad_ref[h])
        s_ref[h % 2] = s
        return jnp.max(s, axis=0, keepdims=True)

    def softmax_pv(j, h, s_max):
        start = pl.multiple_of(j * tk, tk)
        m_old = m_ref[h:h + 1, :]
        m_new = jnp.maximum(m_old, s_max)
        p = jnp.exp2(s_ref[h % 2] - m_new).astype(BF16)
        alpha = jnp.exp2(m_old - m_new)
        kv = h // GROUP
        vt = vt_ref[kv * VT_ROWS:(kv + 1) * VT_ROWS, pl.ds(start, tk)]
        rows = slice(h * VT_ROWS, (h + 1) * VT_ROWS)
        acc_ref[rows, :] = alpha * acc_ref[rows, :] + _dot(vt, p)
        m_ref[h:h + 1, :] = m_new

    def chunk(j, s_max):
        for h in range(Q_HEADS):
            if h + 1 < Q_HEADS:
                nxt = scores(j, h + 1)
            else:
                nxt = scores(jnp.minimum(j + 1, n_chunks - 1), 0)
            softmax_pv(j, h, s_max)
            s_max = nxt
        return s_max

    lax.fori_loop(0, n_chunks, chunk, scores(0, 0), unroll=unroll)

    parts = []
    for h in range(Q_HEADS):
        row_sum = acc_ref[h * VT_ROWS + HEAD_DIM:h * VT_ROWS + HEAD_DIM + 1, :]
        parts.append(acc_ref[h * VT_ROWS:h * VT_ROWS + HEAD_DIM, :] / row_sum)
    out = jnp.concatenate(parts, axis=0).T
    o_ref[...] = _group_rms_norm(out, g_ref[...]).astype(o_ref.dtype)


def _dense_attn(qt, k, vt, g, batch, seq_len):
    tq = min(DENSE_TQ, seq_len)
    tk = min(DENSE_TK, seq_len)
    return pl.pallas_call(
        functools.partial(_dense_attn_kernel, tk, DENSE_UNROLL),
        grid=(batch, seq_len // tq),
        in_specs=[pl.BlockSpec((None, GROUP_WIDTH, tq), lambda b, i: (b, 0, i)),
                  pl.BlockSpec((None, seq_len, KV_WIDTH), lambda b, i: (b, 0, 0)),
                  pl.BlockSpec((None, KV_HEADS * VT_ROWS, seq_len), lambda b, i: (b, 0, 0)),
                  _resident((1, GROUP_WIDTH), lambda b, i: (0, 0))],
        out_specs=pl.BlockSpec((None, tq, GROUP_WIDTH), lambda b, i: (b, i, 0)),
        out_shape=jax.ShapeDtypeStruct((batch, seq_len, GROUP_WIDTH), BF16),
        scratch_shapes=[pltpu.VMEM((Q_HEADS, KV_WIDTH, tq), BF16),
                        pltpu.VMEM((2, tk, tq), F32),
                        pltpu.VMEM((Q_HEADS, tq), F32),
                        pltpu.VMEM((Q_HEADS * VT_ROWS, tq), F32)],
        compiler_params=_params("parallel", "parallel"),
        name="dense_attn",
    )(qt, k, vt, g)


FF_CHUNK = 1024


def _out_mlp_kernel(ma_ref, mb_ref, mc_ref, md_ref, h_ref, wo_ref, g1_ref, b1_ref,
                    w1_ref, w2_ref, g2_ref, b2_ref, o_ref):
    mix = jnp.concatenate([r[...].astype(BF16) for r in (ma_ref, mb_ref, mc_ref, md_ref)], axis=-1)
    h1 = _layer_norm(DN_ALPHA * h_ref[...] + _dot(mix, wo_ref[...]), g1_ref[...], b1_ref[...])
    h1_bf = h1.astype(BF16)
    ffn = jnp.zeros(h1.shape, F32)
    for c in range(D_FF // FF_CHUNK):
        cols = slice(c * FF_CHUNK, (c + 1) * FF_CHUNK)
        u = jnp.maximum(_dot(h1_bf, w1_ref[:, cols]), 0.0)
        ffn = ffn + _dot((u * u).astype(BF16), w2_ref[cols, :])
    o_ref[...] = _layer_norm(DN_ALPHA * h1 + ffn, g2_ref[...], b2_ref[...])


def _out_mlp(mixes, h, w_out, g1, b1, w1, w2, g2, b2):
    rows = h.shape[0]
    tm = min(ROW_TILE, rows)
    row_spec = lambda width: pl.BlockSpec((tm, width), lambda i: (i, 0))
    const = lambda shape: _resident(shape, lambda i: (0, 0))
    vec = const((1, D_MODEL))
    return pl.pallas_call(
        _out_mlp_kernel,
        grid=(rows // tm,),
        in_specs=[row_spec(GROUP_WIDTH)] * 4 + [row_spec(D_MODEL), const((D_MODEL, D_MODEL)),
                                                 vec, vec, const((D_MODEL, D_FF)),
                                                 const((D_FF, D_MODEL)), vec, vec],
        out_specs=row_spec(D_MODEL),
        out_shape=jax.ShapeDtypeStruct((rows, D_MODEL), F32),
        compiler_params=_params("parallel"),
        name="out_mlp",
    )(*mixes, h, w_out, g1, b1, w1, w2, g2, b2)


def kernel(x, ln_in_g, ln_in_b, w_in, conv_w, sink, qn_g, kn_g, grp_g, w_out, ln1_g, ln1_b,
           w1, w2, ln2_g, ln2_b):
    batch, seq_len, _ = x.shape
    rows = batch * seq_len
    depth = w_in.shape[0]
    row_vec = lambda v: v.reshape(1, -1)

    h = x
    for l in range(depth):
        first = l == 0
        outs = _in_proj(h, row_vec(ln_in_g), row_vec(ln_in_b), w_in[l].astype(BF16), qn_g[l],
                        kn_g[l], first, batch, seq_len)
        if first:
            h, *outs = outs
        qta, ka, vta, pb, pc, qtd, kd, vtd = outs
        g = [row_vec(grp_g[l, i * GROUP_WIDTH:(i + 1) * GROUP_WIDTH]) for i in range(4)]
        mix_a = _window_attn(qta, ka, vta, sink[l], g[0], batch, seq_len)
        mix_b = _short_conv(pb, conv_w[l], g[1], batch, seq_len)
        mix_c = _fourier(pc, g[2], batch, seq_len)
        mix_d = _dense_attn(qtd, kd, vtd, g[3], batch, seq_len)
        mixes = [m.reshape(rows, GROUP_WIDTH) for m in (mix_a, mix_b, mix_c, mix_d)]
        h = _out_mlp(mixes, h.reshape(rows, D_MODEL), w_out[l].astype(BF16), row_vec(ln1_g[l]),
                     row_vec(ln1_b[l]), w1[l].astype(BF16), w2[l].astype(BF16), row_vec(ln2_g[l]),
                     row_vec(ln2_b[l])).reshape(batch, seq_len, D_MODEL)
    return h
```

```python
import functools
import math

import numpy as np
import jax
import jax.numpy as jnp
from jax import lax
from jax.experimental import pallas as pl
from jax.experimental.pallas import tpu as pltpu

F32 = jnp.float32
BF16 = jnp.bfloat16

D_MODEL = 1024
HEAD_DIM = 64
GROUP_WIDTH = 256
Q_HEADS = 4
KV_HEADS = 2
GROUP = Q_HEADS // KV_HEADS
KV_WIDTH = KV_HEADS * HEAD_DIM
CONV_WIDTH = 3
WINDOW = 128
BLOCK = 128
GRID_W = 64
AXIS_DIM = HEAD_DIM // 2
ROPE_THETA = 10000.0
D_FF = 4 * D_MODEL
LN_EPS = 1e-5
RMS_EPS = 1e-6
DEPTH = 2
DN_ALPHA = (2 * DEPTH) ** 0.25
NEG_INF = -1e30
LOG2_E = math.log2(math.e)
ALIBI_SLOPES = tuple(2.0 ** (-8.0 * (h + 1) / Q_HEADS) for h in range(Q_HEADS))

V7X_VMEM_LIMIT_BYTES = 56 * 1024 * 1024
LANES = 128
SUBLANES = 8
BF16_SUBLANES = 16
VT_ROWS = HEAD_DIM + BF16_SUBLANES
ROW_TILE = 512


def _params(*semantics):
    return pltpu.CompilerParams(dimension_semantics=semantics,
                                vmem_limit_bytes=V7X_VMEM_LIMIT_BYTES)


def _resident(block_shape, index_map):
    return pl.BlockSpec(block_shape, index_map, pipeline_mode=pl.Buffered(1))


def _dot(a, b):
    return jnp.dot(a, b, preferred_element_type=F32)


def _split_bf16(x):
    hi = x.astype(BF16)
    lo = (x - hi.astype(F32)).astype(BF16)
    return hi, lo


def _dot3(a_hi, a_lo, b_hi, b_lo):
    return _dot(a_hi, b_hi) + _dot(a_hi, b_lo) + _dot(a_lo, b_hi)


def _layer_norm(x, g, b):
    mu = jnp.mean(x, axis=-1, keepdims=True)
    xc = x - mu
    var = jnp.mean(xc * xc, axis=-1, keepdims=True)
    return xc * lax.rsqrt(var + LN_EPS) * g + b


def _group_rms_norm(x, g):
    ms = jnp.mean(x * x, axis=-1, keepdims=True)
    return x * lax.rsqrt(ms + RMS_EPS) * g


A_COLS = 512
B_COLS = 768
C_COLS = 256
D_COLS = 512
D_IN_PROJ = A_COLS + B_COLS + C_COLS + D_COLS


def _rope_tables(seq_len):
    pos = jnp.arange(seq_len)
    inv_freq = ROPE_THETA ** (-jnp.arange(0, AXIS_DIM, 2, dtype=F32) / AXIS_DIM)
    ang_row = (pos // GRID_W).astype(F32)[:, None] * inv_freq[None, :]
    ang_col = (pos % GRID_W).astype(F32)[:, None] * inv_freq[None, :]
    ang = jnp.concatenate([ang_row, ang_row, ang_col, ang_col], axis=-1)
    half = AXIS_DIM // 2
    sign = np.tile(np.concatenate([-np.ones(half), np.ones(half)]), HEAD_DIM // AXIS_DIM)
    cos = jnp.tile(jnp.cos(ang), (1, LANES // HEAD_DIM))
    sin_signed = jnp.tile(jnp.sin(ang) * jnp.asarray(sign, F32), (1, LANES // HEAD_DIM))
    return cos, sin_signed


def _head_norm_rope(x, gain, cos, sin_signed, head_mean):
    width = x.shape[-1]
    sq_hi, sq_lo = _split_bf16(x * x)
    ms = _dot(sq_hi, head_mean) + _dot(sq_lo, head_mean)
    xn = x * lax.rsqrt(ms + RMS_EPS) * gain
    reps = width // LANES
    if reps > 1:
        cos = jnp.concatenate([cos] * reps, axis=-1)
        sin_signed = jnp.concatenate([sin_signed] * reps, axis=-1)
    half = AXIS_DIM // 2
    lane = lax.broadcasted_iota(jnp.int32, x.shape, 1)
    first_half = (lane % AXIS_DIM) < half
    partner = jnp.where(first_half, pltpu.roll(xn, width - half, axis=1),
                        pltpu.roll(xn, half, axis=1))
    return xn * cos + partner * sin_signed


def _store_attn_operands(q, k, v, qt_ref, k_ref, vt_ref):
    rows = q.shape[0]
    qt_ref[...] = (q * (HEAD_DIM ** -0.5 * LOG2_E)).T.astype(BF16)
    k_ref[...] = k.astype(BF16)
    vt = v.T.astype(BF16)
    pad = (lax.broadcasted_iota(jnp.int32, (VT_ROWS - HEAD_DIM, rows), 0) == 0).astype(BF16)
    vt_ref[...] = jnp.concatenate(
        [piece for kv in range(KV_HEADS)
         for piece in (vt[kv * HEAD_DIM:(kv + 1) * HEAD_DIM], pad)], axis=0)


def _in_proj_kernel(apply_ln, x_ref, g_ref, b_ref, w_ref, cos_ref, sin_ref, gq_ref, gk_ref,
                    hm_ref, *out_refs):
    x = x_ref[...]
    if apply_ln:
        h_ref, *out_refs = out_refs
        x = _layer_norm(x, g_ref[...], b_ref[...])
        h_ref[...] = x
    qta_ref, ka_ref, vta_ref, pb_ref, pc_ref, qtd_ref, kd_ref, vtd_ref = out_refs
    proj = _dot(x.astype(BF16), w_ref[...])
    a_q, a_k, a_v = (proj[:, :GROUP_WIDTH], proj[:, GROUP_WIDTH:GROUP_WIDTH + KV_WIDTH],
                     proj[:, GROUP_WIDTH + KV_WIDTH:A_COLS])
    _store_attn_operands(a_q, a_k, a_v, qta_ref, ka_ref, vta_ref)
    pb_ref[...] = proj[:, A_COLS:A_COLS + B_COLS]
    pc_ref[...] = proj[:, A_COLS + B_COLS:A_COLS + B_COLS + C_COLS]
    d0 = A_COLS + B_COLS + C_COLS
    d_q, d_k, d_v = (proj[:, d0:d0 + GROUP_WIDTH],
                     proj[:, d0 + GROUP_WIDTH:d0 + GROUP_WIDTH + KV_WIDTH],
                     proj[:, d0 + GROUP_WIDTH + KV_WIDTH:])
    cos, sin_signed, hm = cos_ref[...], sin_ref[...], hm_ref[...]
    d_q = _head_norm_rope(d_q, gq_ref[...], cos, sin_signed, hm)
    d_k = _head_norm_rope(d_k, gk_ref[...], cos, sin_signed, hm[:KV_WIDTH, :KV_WIDTH])
    _store_attn_operands(d_q, d_k, d_v, qtd_ref, kd_ref, vtd_ref)


def _in_proj(x, g, b, w, qn_g, kn_g, apply_ln, batch, seq_len):
    tm = min(ROW_TILE, seq_len)
    cos, sin_signed = _rope_tables(seq_len)
    gq = jnp.tile(qn_g, GROUP_WIDTH // HEAD_DIM).reshape(1, GROUP_WIDTH)
    gk = jnp.tile(kn_g, KV_WIDTH // HEAD_DIM).reshape(1, KV_WIDTH)
    head_mean = jnp.asarray(
        np.kron(np.eye(GROUP_WIDTH // HEAD_DIM), np.full((HEAD_DIM, HEAD_DIM), 1.0 / HEAD_DIM)),
        BF16)
    rows_spec = lambda width: pl.BlockSpec((None, tm, width), lambda bi, i: (bi, i, 0))
    cols_spec = lambda height: pl.BlockSpec((None, height, tm), lambda bi, i: (bi, 0, i))
    table_spec = pl.BlockSpec((tm, LANES), lambda bi, i: (i, 0))
    const = lambda shape: _resident(shape, lambda bi, i: (0, 0))
    rows_shape = lambda width, dt: jax.ShapeDtypeStruct((batch, seq_len, width), dt)
    cols_shape = lambda height: jax.ShapeDtypeStruct((batch, height, seq_len), BF16)
    attn_specs = [cols_spec(GROUP_WIDTH), rows_spec(KV_WIDTH), cols_spec(KV_HEADS * VT_ROWS)]
    attn_shapes = [cols_shape(GROUP_WIDTH), rows_shape(KV_WIDTH, BF16),
                   cols_shape(KV_HEADS * VT_ROWS)]
    out_specs = attn_specs + [rows_spec(B_COLS), rows_spec(C_COLS)] + attn_specs
    out_shape = attn_shapes + [rows_shape(B_COLS, F32), rows_shape(C_COLS, F32)] + attn_shapes
    if apply_ln:
        out_specs = [rows_spec(D_MODEL)] + out_specs
        out_shape = [rows_shape(D_MODEL, F32)] + out_shape
    return pl.pallas_call(
        functools.partial(_in_proj_kernel, apply_ln),
        grid=(batch, seq_len // tm),
        in_specs=[rows_spec(D_MODEL), const((1, D_MODEL)), const((1, D_MODEL)),
                  const((D_MODEL, D_IN_PROJ)), table_spec, table_spec,
                  const((1, GROUP_WIDTH)), const((1, KV_WIDTH)), const((GROUP_WIDTH, GROUP_WIDTH))],
        out_specs=out_specs,
        out_shape=out_shape,
        compiler_params=_params("parallel", "parallel"),
        name="in_proj_ln" if apply_ln else "in_proj",
    )(x, g, b, w, cos, sin_signed, gq, gk, head_mean)


WINDOW_TQ = 2048


def _window_bias_table():
    key = np.arange(3 * BLOCK)[:, None]
    query = np.arange(BLOCK)[None, :]
    dist = np.abs(BLOCK + query - key)
    per_head = [np.where(dist <= WINDOW, -slope * LOG2_E * dist, NEG_INF) for slope in ALIBI_SLOPES]
    groups = [np.concatenate(per_head[kv * GROUP:(kv + 1) * GROUP], axis=1) for kv in range(KV_HEADS)]
    return jnp.asarray(np.stack(groups), F32)


def _window_attn_kernel(qt_ref, km_ref, kp_ref, kn_ref, vm_ref, vp_ref, vn_ref, bias_ref,
                        sink_ref, g_ref, o_ref, kall_ref, vall_ref, s_ref):
    tq = qt_ref.shape[1]
    n_blk = tq // BLOCK
    i = pl.program_id(1)
    before_start = jnp.where(i == 0, NEG_INF, 0.0)
    after_end = jnp.where(i == pl.num_programs(1) - 1, NEG_INF, 0.0)

    kall_ref[0:BLOCK, :] = kp_ref[...]
    kall_ref[BLOCK:BLOCK + tq, :] = km_ref[...]
    kall_ref[BLOCK + tq:, :] = kn_ref[...]
    vall_ref[:, 0:BLOCK] = vp_ref[...]
    vall_ref[:, BLOCK:BLOCK + tq] = vm_ref[...]
    vall_ref[:, BLOCK + tq:] = vn_ref[...]
    zeros = jnp.zeros((HEAD_DIM, BLOCK), BF16)
    g = g_ref[...]

    def scores(jb, kv):
        cols = slice(jb * BLOCK, (jb + 1) * BLOCK)
        halves = []
        for h in range(kv * GROUP, (kv + 1) * GROUP):
            qh = qt_ref[h * HEAD_DIM:(h + 1) * HEAD_DIM, cols]
            halves.append(jnp.concatenate([qh, zeros] if kv == 0 else [zeros, qh], axis=0))
        kw = kall_ref[jb * BLOCK:(jb + 3) * BLOCK, :]
        s = _dot(kw, jnp.concatenate(halves, axis=1)) + bias_ref[kv]
        if jb == 0:
            s = jnp.concatenate([s[:BLOCK] + before_start, s[BLOCK:]], axis=0)
        if jb == n_blk - 1:
            s = jnp.concatenate([s[:2 * BLOCK], s[2 * BLOCK:] + after_end], axis=0)
        s_ref[kv] = s
        return jnp.max(s, axis=0, keepdims=True)

    def softmax_pv(jb, kv, s_max):
        sink = sink_ref[kv]
        m = jnp.maximum(s_max, sink)
        p = jnp.exp2(s_ref[kv] - m).astype(BF16)
        vt = vall_ref[kv * VT_ROWS:(kv + 1) * VT_ROWS, jb * BLOCK:(jb + 3) * BLOCK]
        return _dot(vt, p), jnp.exp2(sink - m)

    outs = []

    def finish(jb, kv, pv, sink_term):
        o = pv[:HEAD_DIM] / (pv[HEAD_DIM:HEAD_DIM + 1] + sink_term)
        outs.extend([o[:, :BLOCK], o[:, BLOCK:]])
        if kv == KV_HEADS - 1:
            out = jnp.concatenate(outs, axis=0).T
            o_ref[jb * BLOCK:(jb + 1) * BLOCK, :] = _group_rms_norm(out, g).astype(o_ref.dtype)
            outs.clear()

    units = [(jb, kv) for jb in range(n_blk) for kv in range(KV_HEADS)]
    s_max = scores(*units[0])
    pending = None
    for idx, unit in enumerate(units):
        nxt = scores(*units[idx + 1]) if idx + 1 < len(units) else None
        pv_parts = softmax_pv(*unit, s_max)
        if pending is not None:
            finish(*pending)
        pending = (*unit, *pv_parts)
        s_max = nxt
    finish(*pending)


def _window_attn(qt, k, vt, sink, g, batch, seq_len):
    tq = min(WINDOW_TQ, seq_len)
    per_tile = tq // BLOCK
    n_blocks = seq_len // BLOCK
    sink_rows = jnp.repeat(sink * LOG2_E, BLOCK).reshape(KV_HEADS, 1, GROUP * BLOCK)
    prev_idx = lambda i: jnp.maximum(i * per_tile - 1, 0)
    next_idx = lambda i: jnp.minimum((i + 1) * per_tile, n_blocks - 1)
    v_rows = KV_HEADS * VT_ROWS
    const3 = lambda shape: _resident(shape, lambda b, i: (0, 0, 0))
    return pl.pallas_call(
        _window_attn_kernel,
        grid=(batch, seq_len // tq),
        in_specs=[pl.BlockSpec((None, GROUP_WIDTH, tq), lambda b, i: (b, 0, i)),
                  pl.BlockSpec((None, tq, KV_WIDTH), lambda b, i: (b, i, 0)),
                  pl.BlockSpec((None, BLOCK, KV_WIDTH), lambda b, i: (b, prev_idx(i), 0)),
                  pl.BlockSpec((None, BLOCK, KV_WIDTH), lambda b, i: (b, next_idx(i), 0)),
                  pl.BlockSpec((None, v_rows, tq), lambda b, i: (b, 0, i)),
                  pl.BlockSpec((None, v_rows, BLOCK), lambda b, i: (b, 0, prev_idx(i))),
                  pl.BlockSpec((None, v_rows, BLOCK), lambda b, i: (b, 0, next_idx(i))),
                  const3((KV_HEADS, 3 * BLOCK, GROUP * BLOCK)),
                  const3((KV_HEADS, 1, GROUP * BLOCK)),
                  _resident((1, GROUP_WIDTH), lambda b, i: (0, 0))],
        out_specs=pl.BlockSpec((None, tq, GROUP_WIDTH), lambda b, i: (b, i, 0)),
        out_shape=jax.ShapeDtypeStruct((batch, seq_len, GROUP_WIDTH), BF16),
        scratch_shapes=[pltpu.VMEM((tq + 2 * BLOCK, KV_WIDTH), BF16),
                        pltpu.VMEM((v_rows, tq + 2 * BLOCK), BF16),
                        pltpu.VMEM((KV_HEADS, 3 * BLOCK, GROUP * BLOCK), F32)],
        compiler_params=_params("parallel", "parallel"),
        name="window_attn",
    )(qt, k, k, k, vt, vt, vt, _window_bias_table(), sink_rows, g)


def _short_conv_kernel(u_ref, gb_ref, gc_ref, up_ref, un_ref, cp_ref, cn_ref, w_ref, g_ref,
                       o_ref):
    tm = u_ref.shape[0]
    i = pl.program_id(1)
    last = pl.num_programs(1) - 1
    z = gc_ref[...] * u_ref[...]
    z_before = jnp.where(i > 0, cp_ref[SUBLANES - 1:SUBLANES, :] * up_ref[SUBLANES - 1:SUBLANES, :],
                         0.0)
    z_after = jnp.where(i < last, cn_ref[0:1, :] * un_ref[0:1, :], 0.0)
    row = lax.broadcasted_iota(jnp.int32, z.shape, 0)
    z_m1 = jnp.where(row == 0, z_before, pltpu.roll(z, 1, axis=0))
    z_p1 = jnp.where(row == tm - 1, z_after, pltpu.roll(z, tm - 1, axis=0))
    w = w_ref[...]
    y = gb_ref[...] * (w[0:1, :] * z_m1 + w[1:2, :] * z + w[2:3, :] * z_p1)
    o_ref[...] = _group_rms_norm(y, g_ref[...]).astype(o_ref.dtype)


def _short_conv(pb, conv_w, g, batch, seq_len):
    tm = min(ROW_TILE, seq_len)
    per_tile = tm // SUBLANES
    n_row_blocks = seq_len // SUBLANES
    u_col, gb_col, gc_col = 0, 1, 2

    def main(col):
        return pl.BlockSpec((None, tm, GROUP_WIDTH), lambda b, i: (b, i, col))

    def prev(col):
        return pl.BlockSpec((None, SUBLANES, GROUP_WIDTH),
                            lambda b, i: (b, jnp.maximum(i * per_tile - 1, 0), col))

    def nxt(col):
        return pl.BlockSpec((None, SUBLANES, GROUP_WIDTH),
                            lambda b, i: (b, jnp.minimum((i + 1) * per_tile, n_row_blocks - 1), col))

    return pl.pallas_call(
        _short_conv_kernel,
        grid=(batch, seq_len // tm),
        in_specs=[main(u_col), main(gb_col), main(gc_col),
                  prev(u_col), nxt(u_col), prev(gc_col), nxt(gc_col),
                  _resident((CONV_WIDTH, GROUP_WIDTH), lambda b, i: (0, 0)),
                  _resident((1, GROUP_WIDTH), lambda b, i: (0, 0))],
        out_specs=pl.BlockSpec((None, tm, GROUP_WIDTH), lambda b, i: (b, i, 0)),
        out_shape=jax.ShapeDtypeStruct((batch, seq_len, GROUP_WIDTH), BF16),
        compiler_params=_params("parallel", "parallel"),
        name="short_conv",
    )(pb, pb, pb, pb, pb, pb, pb, conv_w, g)


FOURIER_T2_PER_STEP = 8
FOURIER_K1_PER_STEP = SUBLANES


def _dft_cos_sin(n):
    idx = np.arange(n)
    ang = 2.0 * np.pi * ((idx[:, None] * idx[None, :]) % n) / n
    return np.cos(ang), np.sin(ang)


def _fourier_stage1_kernel(x_ref, wh_ref, wl_ref, tc_ref, ts_ref, o_ref):
    n1 = x_ref.shape[0]
    x_hi, x_lo = _split_bf16(x_ref[...])
    a = _dot3(wh_ref[...], wl_ref[...], x_hi, x_lo)
    a_re, a_im = a[:n1], a[n1:]
    tc, ts = tc_ref[...], ts_ref[...]
    o_ref[0] = a_re * tc + a_im * ts
    o_ref[1] = a_im * tc - a_re * ts


def _fourier_stage2_kernel(a_ref, w2h_ref, w2l_ref, w3h_ref, w3l_ref, g_ref, o_ref):
    n2 = a_ref.shape[2]
    for j in range(a_ref.shape[1]):
        a = jnp.concatenate([a_ref[0, j], a_ref[1, j]], axis=0)
        a_hi, a_lo = _split_bf16(a)
        y = _dot3(w2h_ref[...], w2l_ref[...], a_hi, a_lo)
        y2 = jnp.concatenate([y[:n2], y[n2:]], axis=1)
        y_hi, y_lo = _split_bf16(y2)
        z = _dot3(y_hi, y_lo, w3h_ref[...], w3l_ref[...])
        o_ref[:, j, :] = _group_rms_norm(z, g_ref[...]).astype(o_ref.dtype)


def _fourier(pc, g, batch, seq_len):
    n = math.isqrt(seq_len)
    assert n * n == seq_len and n % FOURIER_T2_PER_STEP == 0
    cols = FOURIER_T2_PER_STEP * GROUP_WIDTH

    cos_n, sin_n = _dft_cos_sin(n)
    w1_hi, w1_lo = _split_bf16(jnp.asarray(np.concatenate([cos_n, -sin_n], axis=0), F32))
    w2 = np.block([[cos_n, sin_n], [-sin_n, cos_n]])
    w2_hi, w2_lo = _split_bf16(jnp.asarray(w2, F32))
    cos_c, sin_c = _dft_cos_sin(HEAD_DIM)
    eye = np.eye(GROUP_WIDTH // HEAD_DIM)
    w3 = np.concatenate([np.kron(eye, cos_c), np.kron(eye, sin_c)], axis=0)
    w3_hi, w3_lo = _split_bf16(jnp.asarray(w3 / math.sqrt(seq_len * HEAD_DIM), F32))

    k1 = jnp.arange(n, dtype=jnp.int32)[:, None]
    t2 = jnp.arange(n, dtype=jnp.int32)[None, :]
    theta = (2.0 * math.pi / seq_len) * (k1 * t2).astype(F32)
    tw_cos = jnp.repeat(jnp.cos(theta), GROUP_WIDTH, axis=1)
    tw_sin = jnp.repeat(jnp.sin(theta), GROUP_WIDTH, axis=1)

    const = lambda shape: _resident(shape, lambda b, j: (0, 0))
    stage1 = pl.pallas_call(
        _fourier_stage1_kernel,
        grid=(batch, n // FOURIER_T2_PER_STEP),
        in_specs=[pl.BlockSpec((None, n, cols), lambda b, j: (b, 0, j)),
                  const((2 * n, n)), const((2 * n, n)),
                  pl.BlockSpec((n, cols), lambda b, j: (0, j)),
                  pl.BlockSpec((n, cols), lambda b, j: (0, j))],
        out_specs=pl.BlockSpec((None, 2, n, cols), lambda b, j: (b, 0, 0, j)),
        out_shape=jax.ShapeDtypeStruct((batch, 2, n, n * GROUP_WIDTH), F32),
        compiler_params=_params("parallel", "parallel"),
        name="fourier_stage1",
    )(pc.reshape(batch, n, n * GROUP_WIDTH), w1_hi, w1_lo, tw_cos, tw_sin)

    out = pl.pallas_call(
        _fourier_stage2_kernel,
        grid=(batch, n // FOURIER_K1_PER_STEP),
        in_specs=[pl.BlockSpec((None, 2, FOURIER_K1_PER_STEP, n, GROUP_WIDTH),
                               lambda b, j: (b, 0, j, 0, 0)),
                  const((2 * n, 2 * n)), const((2 * n, 2 * n)),
                  const((2 * GROUP_WIDTH, GROUP_WIDTH)), const((2 * GROUP_WIDTH, GROUP_WIDTH)),
                  const((1, GROUP_WIDTH))],
        out_specs=pl.BlockSpec((None, n, FOURIER_K1_PER_STEP, GROUP_WIDTH),
                               lambda b, j: (b, 0, j, 0)),
        out_shape=jax.ShapeDtypeStruct((batch, n, n, GROUP_WIDTH), F32),
        compiler_params=_params("parallel", "parallel"),
        name="fourier_stage2",
    )(stage1.reshape(batch, 2, n, n, GROUP_WIDTH), w2_hi, w2_lo, w3_hi, w3_lo, g)
    return out.reshape(batch, seq_len, GROUP_WIDTH)


DENSE_TQ = 512
DENSE_TK = 1024
DENSE_UNROLL = 4


def _dense_attn_kernel(tk, unroll, qt_ref, k_ref, vt_ref, g_ref, o_ref, qpad_ref, s_ref, m_ref,
                       acc_ref):
    tq = qt_ref.shape[1]
    n_chunks = k_ref.shape[0] // tk

    m_ref[...] = jnp.full(m_ref.shape, -jnp.inf, F32)
    acc_ref[...] = jnp.zeros(acc_ref.shape, F32)
    zeros = jnp.zeros((HEAD_DIM, tq), BF16)
    for h in range(Q_HEADS):
        qh = qt_ref[h * HEAD_DIM:(h + 1) * HEAD_DIM, :]
        qpad_ref[h] = jnp.concatenate([qh, zeros] if h < GROUP else [zeros, qh], axis=0)

    def scores(j, h):
        start = pl.multiple_of(j * tk, tk)
        s = _dot(k_ref[pl.ds(start, tk), :], qpad_ref[h])
        s_ref[h % 2] = s
        return jnp.max(s, axis=0, keepdims=True)

    def softmax_pv(j, h, s_max):
        start = pl.multiple_of(j * tk, tk)
        m_old = m_ref[h:h + 1, :]
        m_new = jnp.maximum(m_old, s_max)
        p = jnp.exp2(s_ref[h % 2] - m_new).astype(BF16)
        alpha = jnp.exp2(m_old - m_new)
        kv = h // GROUP
        vt = vt_ref[kv * VT_ROWS:(kv + 1) * VT_ROWS, pl.ds(start, tk)]
        rows = slice(h * VT_ROWS, (h + 1) * VT_ROWS)
        acc_ref[rows, :] = alpha * acc_ref[rows, :] + _dot(vt, p)
        m_ref[h:h + 1, :] = m_new

    def chunk(j, s_max):
        for h in range(Q_HEADS):
            if h + 1 < Q_HEADS:
                nxt = scores(j, h + 1)
            else:
                nxt = scores(jnp.minimum(j + 1, n_chunks - 1), 0)
            softmax_pv(j, h, s_max)
            s_max = nxt
        return s_max

    lax.fori_loop(0, n_chunks, chunk, scores(0, 0), unroll=unroll)

    parts = []
    for h in range(Q_HEADS):
        row_sum = acc_ref[h * VT_ROWS + HEAD_DIM:h * VT_ROWS + HEAD_DIM + 1, :]
        parts.append(acc_ref[h * VT_ROWS:h * VT_ROWS + HEAD_DIM, :] / row_sum)
    out = jnp.concatenate(parts, axis=0).T
    o_ref[...] = _group_rms_norm(out, g_ref[...]).astype(o_ref.dtype)


def _dense_attn(qt, k, vt, g, batch, seq_len):
    tq = min(DENSE_TQ, seq_len)
    tk = min(DENSE_TK, seq_len)
    return pl.pallas_call(
        functools.partial(_dense_attn_kernel, tk, DENSE_UNROLL),
        grid=(batch, seq_len // tq),
        in_specs=[pl.BlockSpec((None, GROUP_WIDTH, tq), lambda b, i: (b, 0, i)),
                  pl.BlockSpec((None, seq_len, KV_WIDTH), lambda b, i: (b, 0, 0)),
                  pl.BlockSpec((None, KV_HEADS * VT_ROWS, seq_len), lambda b, i: (b, 0, 0)),
                  _resident((1, GROUP_WIDTH), lambda b, i: (0, 0))],
        out_specs=pl.BlockSpec((None, tq, GROUP_WIDTH), lambda b, i: (b, i, 0)),
        out_shape=jax.ShapeDtypeStruct((batch, seq_len, GROUP_WIDTH), BF16),
        scratch_shapes=[pltpu.VMEM((Q_HEADS, KV_WIDTH, tq), BF16),
                        pltpu.VMEM((2, tk, tq), F32),
                        pltpu.VMEM((Q_HEADS, tq), F32),
                        pltpu.VMEM((Q_HEADS * VT_ROWS, tq), F32)],
        compiler_params=_params("parallel", "parallel"),
        name="dense_attn",
    )(qt, k, vt, g)


FF_CHUNK = 1024


def _out_mlp_kernel(ma_ref, mb_ref, mc_ref, md_ref, h_ref, wo_ref, g1_ref, b1_ref,
                    w1_ref, w2_ref, g2_ref, b2_ref, o_ref):
    mix = jnp.concatenate([r[...].astype(BF16) for r in (ma_ref, mb_ref, mc_ref, md_ref)], axis=-1)
    h1 = _layer_norm(DN_ALPHA * h_ref[...] + _dot(mix, wo_ref[...]), g1_ref[...], b1_ref[...])
    h1_bf = h1.astype(BF16)
    ffn = jnp.zeros(h1.shape, F32)
    for c in range(D_FF // FF_CHUNK):
        cols = slice(c * FF_CHUNK, (c + 1) * FF_CHUNK)
        u = jnp.maximum(_dot(h1_bf, w1_ref[:, cols]), 0.0)
        ffn = ffn + _dot((u * u).astype(BF16), w2_ref[cols, :])
    o_ref[...] = _layer_norm(DN_ALPHA * h1 + ffn, g2_ref[...], b2_ref[...])


def _out_mlp(mixes, h, w_out, g1, b1, w1, w2, g2, b2):
    rows = h.shape[0]
    tm = min(ROW_TILE, rows)
    row_spec = lambda width: pl.BlockSpec((tm, width), lambda i: (i, 0))
    const = lambda shape: _resident(shape, lambda i: (0, 0))
    vec = const((1, D_MODEL))
    return pl.pallas_call(
        _out_mlp_kernel,
        grid=(rows // tm,),
        in_specs=[row_spec(GROUP_WIDTH)] * 4 + [row_spec(D_MODEL), const((D_MODEL, D_MODEL)),
                                                 vec, vec, const((D_MODEL, D_FF)),
                                                 const((D_FF, D_MODEL)), vec, vec],
        out_specs=row_spec(D_MODEL),
        out_shape=jax.ShapeDtypeStruct((rows, D_MODEL), F32),
        compiler_params=_params("parallel"),
        name="out_mlp",
    )(*mixes, h, w_out, g1, b1, w1, w2, g2, b2)


def kernel(x, ln_in_g, ln_in_b, w_in, conv_w, sink, qn_g, kn_g, grp_g, w_out, ln1_g, ln1_b,
           w1, w2, ln2_g, ln2_b):
    batch, seq_len, _ = x.shape
    rows = batch * seq_len
    depth = w_in.shape[0]
    row_vec = lambda v: v.reshape(1, -1)

    h = x
    for l in range(depth):
        first = l == 0
        outs = _in_proj(h, row_vec(ln_in_g), row_vec(ln_in_b), w_in[l].astype(BF16), qn_g[l],
                        kn_g[l], first, batch, seq_len)
        if first:
            h, *outs = outs
        qta, ka, vta, pb, pc, qtd, kd, vtd = outs
        g = [row_vec(grp_g[l, i * GROUP_WIDTH:(i + 1) * GROUP_WIDTH]) for i in range(4)]
        mix_a = _window_attn(qta, ka, vta, sink[l], g[0], batch, seq_len)
        mix_b = _short_conv(pb, conv_w[l], g[1], batch, seq_len)
        mix_c = _fourier(pc, g[2], batch, seq_len)
        mix_d = _dense_attn(qtd, kd, vtd, g[3], batch, seq_len)
        mixes = [m.reshape(rows, GROUP_WIDTH) for m in (mix_a, mix_b, mix_c, mix_d)]
        h = _out_mlp(mixes, h.reshape(rows, D_MODEL), w_out[l].astype(BF16), row_vec(ln1_g[l]),
                     row_vec(ln1_b[l]), w1[l].astype(BF16), w2[l].astype(BF16), row_vec(ln2_g[l]),
                     row_vec(ln2_b[l])).reshape(batch, seq_len, D_MODEL)
    return h
```
